```python
import jax, jax.numpy as jnp
from jax import lax
import numpy as np

D_MODEL = 1024
BATCH = 2
SEQ = 8192
DEPTH = 2
DEC_BATCH = 4
DEC_SEQ = 4096
PAST_LEN = 128

MIX_WIDTH = D_MODEL
HEAD_DIM = 64
D_FOURIER = MIX_WIDTH // 2
D_LRU = MIX_WIDTH - D_FOURIER
N_FOURIER_HEADS = D_FOURIER // HEAD_DIM
N_LRU_HEADS = D_LRU // HEAD_DIM
D_IN = D_FOURIER + 2 * D_LRU
CONV_WIDTH = 4
CONV_PAD_LEFT = 2
CONV_PAD_RIGHT = CONV_WIDTH - 1 - CONV_PAD_LEFT
LRU_C = 8.0
D_FF = 4 * D_MODEL
N_MOD = 6
EPS = 1e-6

kernel_name = "hybrid_fnet_rglru_adaln_encoder"


def rms_norm(x, g):
    xf = x.astype(jnp.float32)
    y = xf * lax.rsqrt(jnp.mean(xf * xf, axis=-1, keepdims=True) + EPS)
    return (y * g.astype(jnp.float32)).astype(x.dtype)


def fourier_mixer(u, w_f):
    b, s, _ = u.shape
    uh = u.reshape(b, s, N_FOURIER_HEADS, HEAD_DIM).astype(jnp.float32)
    f = jnp.fft.fft2(uh, axes=(1, 3), norm="ortho").real.astype(u.dtype)
    y = jnp.einsum("bshd,hde->bshe", f, w_f)
    return y.reshape(b, s, D_FOURIER)


def centred_depthwise_conv(u, w, bias):
    s = u.shape[1]
    up = jnp.pad(u, ((0, 0), (CONV_PAD_LEFT, CONV_PAD_RIGHT), (0, 0)))
    out = bias
    for k in range(CONV_WIDTH):
        out = out + up[:, k:k + s, :] * w[k]
    return out


def _lru_combine(e1, e2):
    a1, b1 = e1
    a2, b2 = e2
    return a1 * a2, a2 * b1 + b2


def rglru_direction(x, w_r, b_r, w_i, b_i, lam, reverse):
    b, s, _ = x.shape
    xh = x.reshape(b, s, N_LRU_HEADS, HEAD_DIM)
    r = jax.nn.sigmoid(jnp.einsum("bshd,hde->bshe", xh, w_r) + b_r).reshape(b, s, D_LRU)
    i = jax.nn.sigmoid(jnp.einsum("bshd,hde->bshe", xh, w_i) + b_i).reshape(b, s, D_LRU)
    log_a = -LRU_C * r * jax.nn.softplus(-lam)
    a = jnp.exp(log_a)
    mult = jnp.sqrt(jnp.maximum(-jnp.expm1(2.0 * log_a), 0.0))
    bx = mult * (i * x)
    _, h = lax.associative_scan(_lru_combine, (a, bx), axis=1, reverse=reverse)
    return h


def bidirectional_rglru(u, w_rg, b_rg, w_ig, b_ig, lam):
    uf = u.astype(jnp.float32)
    w_rg = w_rg.astype(jnp.float32); b_rg = b_rg.astype(jnp.float32)
    w_ig = w_ig.astype(jnp.float32); b_ig = b_ig.astype(jnp.float32)
    lam = lam.astype(jnp.float32)
    h_fwd = rglru_direction(uf, w_rg[0], b_rg[0], w_ig[0], b_ig[0], lam[0], reverse=False)
    h_bwd = rglru_direction(uf, w_rg[1], b_rg[1], w_ig[1], b_ig[1], lam[1], reverse=True)
    return (h_fwd + h_bwd).astype(u.dtype)


def encoder_layer(x, c, g_mix, g_mlp, w_mod, b_mod, w_in, w_fourier, conv_w, conv_b,
                  w_rgate, b_rgate, w_igate, b_igate, lru_lambda, w_out, w_ff1, w_ff2):
    mod = jax.nn.silu(c.astype(jnp.float32)) @ w_mod.astype(jnp.float32) + b_mod.astype(jnp.float32)
    mod = mod.astype(x.dtype)[:, None, :]
    shift1, scale1, gate1, shift2, scale2, gate2 = jnp.split(mod, N_MOD, axis=-1)

    h = rms_norm(x, g_mix) * (1.0 + scale1) + shift1
    z = h @ w_in
    u_f = z[..., :D_FOURIER]
    u_r = z[..., D_FOURIER:D_FOURIER + D_LRU]
    u_g = z[..., D_FOURIER + D_LRU:]
    y_f = fourier_mixer(u_f, w_fourier)
    u_c = centred_depthwise_conv(u_r, conv_w, conv_b)
    y_r = bidirectional_rglru(u_c, w_rgate, b_rgate, w_igate, b_igate, lru_lambda) * jax.nn.gelu(u_g)
    mix = jnp.concatenate([y_f, y_r], axis=-1) @ w_out
    x = x + gate1 * mix

    h = rms_norm(x, g_mlp) * (1.0 + scale2) + shift2
    ff = jnp.square(jax.nn.relu(h @ w_ff1)) @ w_ff2
    return x + gate2 * ff


def trunk(x, c, g_mix, g_mlp, w_mod, b_mod, w_in, w_fourier, conv_w, conv_b,
          w_rgate, b_rgate, w_igate, b_igate, lru_lambda, w_out, w_ff1, w_ff2, g_final):
    for l in range(DEPTH):
        x = encoder_layer(x, c, g_mix[l], g_mlp[l], w_mod[l], b_mod[l], w_in[l], w_fourier[l],
                          conv_w[l], conv_b[l], w_rgate[l], b_rgate[l], w_igate[l], b_igate[l],
                          lru_lambda[l], w_out[l], w_ff1[l], w_ff2[l])
    return rms_norm(x, g_final)


def setup_inputs(seed: int = 0) -> dict:
    key = jax.random.key(seed)
    ks = jax.random.split(key, 24)
    f32 = jnp.float32
    nrm = lambda k, shape, scale: (jax.random.normal(k, shape, f32) * scale).astype(f32)
    a0 = jax.random.uniform(ks[18], (DEPTH, 2, D_LRU), f32, 0.9, 0.999)
    return {
        "x_prompt": nrm(ks[0], (BATCH, SEQ, D_MODEL), 1.0),
        "x_sample": nrm(ks[1], (DEC_BATCH, DEC_SEQ, D_MODEL), 1.0),
        "c_prompt": nrm(ks[2], (BATCH, D_MODEL), 1.0),
        "c_sample": nrm(ks[3], (DEC_BATCH, D_MODEL), 1.0),
        "g_mix": 1.0 + nrm(ks[4], (DEPTH, D_MODEL), 0.02),
        "g_mlp": 1.0 + nrm(ks[5], (DEPTH, D_MODEL), 0.02),
        "w_mod": nrm(ks[6], (DEPTH, D_MODEL, N_MOD * D_MODEL), 0.5 * D_MODEL ** -0.5),
        "b_mod": nrm(ks[7], (DEPTH, N_MOD * D_MODEL), 0.02),
        "w_in": nrm(ks[8], (DEPTH, D_MODEL, D_IN), D_MODEL ** -0.5),
        "w_fourier": nrm(ks[9], (DEPTH, N_FOURIER_HEADS, HEAD_DIM, HEAD_DIM), HEAD_DIM ** -0.5),
        "conv_w": nrm(ks[10], (DEPTH, CONV_WIDTH, D_LRU), 0.5),
        "conv_b": nrm(ks[11], (DEPTH, D_LRU), 0.02),
        "w_rgate": nrm(ks[12], (DEPTH, 2, N_LRU_HEADS, HEAD_DIM, HEAD_DIM), HEAD_DIM ** -0.5),
        "b_rgate": nrm(ks[13], (DEPTH, 2, N_LRU_HEADS, HEAD_DIM), 0.02),
        "w_igate": nrm(ks[14], (DEPTH, 2, N_LRU_HEADS, HEAD_DIM, HEAD_DIM), HEAD_DIM ** -0.5),
        "b_igate": nrm(ks[15], (DEPTH, 2, N_LRU_HEADS, HEAD_DIM), 0.02),
        "lru_lambda": jnp.log(a0) - jnp.log1p(-a0),
        "w_out": nrm(ks[16], (DEPTH, MIX_WIDTH, D_MODEL), MIX_WIDTH ** -0.5),
        "w_ff1": nrm(ks[17], (DEPTH, D_MODEL, D_FF), D_MODEL ** -0.5),
        "w_ff2": nrm(ks[19], (DEPTH, D_FF, D_MODEL), D_FF ** -0.5),
        "g_final": 1.0 + nrm(ks[20], (D_MODEL,), 0.02),
    }


def reference(x_prompt, x_sample, c_prompt, c_sample, g_mix, g_mlp, w_mod, b_mod, w_in,
              w_fourier, conv_w, conv_b, w_rgate, b_rgate, w_igate, b_igate, lru_lambda,
              w_out, w_ff1, w_ff2, g_final):
    y_prompt = trunk(x_prompt, c_prompt, g_mix, g_mlp, w_mod, b_mod, w_in, w_fourier, conv_w,
                     conv_b, w_rgate, b_rgate, w_igate, b_igate, lru_lambda, w_out, w_ff1,
                     w_ff2, g_final)
    y_sample = trunk(x_sample, c_sample, g_mix, g_mlp, w_mod, b_mod, w_in, w_fourier, conv_w,
                     conv_b, w_rgate, b_rgate, w_igate, b_igate, lru_lambda, w_out, w_ff1,
                     w_ff2, g_final)
    return (y_prompt, y_sample)
```

```python
import functools
import math

import numpy as np
import jax
import jax.numpy as jnp
from jax import lax
from jax.experimental import pallas as pl
from jax.experimental.pallas import tpu as pltpu

F32 = jnp.float32
BF16 = jnp.bfloat16
HIGHEST = lax.Precision.HIGHEST

D_MODEL = 1024
HEAD_DIM = 64
D_FOURIER = 512
D_LRU = 512
N_HEADS = 8
D_FF = 4096
N_MOD = 6
EPS = 1e-6
LRU_C = 8.0
CONV_WIDTH = 4

LANES = 128
SUBLANES = 8
BF16_ROWS = 16
N_SLAB = D_LRU // LANES
ROW_TILE = 512
SEG = ROW_TILE // SUBLANES
DFT_N2 = 256
DFT_GROUP = BF16_ROWS
FF_CHUNK = 1024
MOD_COLS = 1536
VMEM_LIMIT = 56 * 1024 * 1024


def _const_spec(shape):
    n = len(shape)
    return pl.BlockSpec(shape, lambda *_: (0,) * n, pipeline_mode=pl.Buffered(1))


def _params(*sem):
    return pltpu.CompilerParams(dimension_semantics=sem, vmem_limit_bytes=VMEM_LIMIT)


def _mod_kernel(c_ref, w_ref, b_ref, o_ref):
    c = c_ref[...]
    s = c * jax.nn.sigmoid(c)
    o_ref[...] = jnp.dot(s, w_ref[...], precision=HIGHEST, preferred_element_type=F32) + b_ref[...]


def _modulation(c_all, w_mod, b_mod):
    depth = w_mod.shape[0]
    n_out = N_MOD * D_MODEL
    rows = c_all.shape[0]
    return pl.pallas_call(
        _mod_kernel,
        grid=(depth, n_out // MOD_COLS),
        in_specs=[
            pl.BlockSpec((rows, D_MODEL), lambda l, j: (0, 0)),
            pl.BlockSpec((None, D_MODEL, MOD_COLS), lambda l, j: (l, 0, j)),
            pl.BlockSpec((None, 1, MOD_COLS), lambda l, j: (l, 0, j)),
        ],
        out_specs=pl.BlockSpec((None, rows, MOD_COLS), lambda l, j: (l, 0, j)),
        out_shape=jax.ShapeDtypeStruct((depth, rows, n_out), F32),
        compiler_params=_params("parallel", "parallel"),
        name="adaln_mod",
    )(c_all, w_mod, b_mod.reshape(depth, 1, n_out))


def _prep_kernel(win_ref, bdwf_ref, bdc_ref, bds_ref, o_ref):
    wf = bdwf_ref[...]
    ma = jnp.dot(bdc_ref[...], wf, precision=HIGHEST, preferred_element_type=F32)
    mv = -jnp.dot(bds_ref[...], wf, precision=HIGHEST, preferred_element_type=F32)
    wi = win_ref[:, 0:D_FOURIER]
    o_ref[:, 0:D_FOURIER] = jnp.dot(wi, ma, precision=HIGHEST, preferred_element_type=F32).astype(BF16)
    o_ref[:, D_FOURIER:2 * D_FOURIER] = jnp.dot(
        wi, mv, precision=HIGHEST, preferred_element_type=F32).astype(BF16)
    o_ref[:, 2 * D_FOURIER:] = win_ref[:, D_FOURIER:].astype(BF16)


def _block_diag(w):
    h, d, e = w.shape[-3:]
    eye = jnp.eye(h, dtype=w.dtype)
    out = w[..., :, :, None, :] * eye[:, None, :, None]
    return out.reshape(w.shape[:-3] + (h * d, h * e))


def _channel_dft_tables():
    k = np.arange(HEAD_DIM)
    ang = 2.0 * np.pi * np.outer(k, k) / HEAD_DIM
    eye = np.eye(N_HEADS)
    return (jnp.asarray(np.kron(eye, np.cos(ang)), F32), jnp.asarray(np.kron(eye, np.sin(ang)), F32))


def _fold_in_weights(w_in, w_fourier):
    depth = w_in.shape[0]
    d_in = w_in.shape[2]
    n_out = d_in + D_FOURIER
    bdc, bds = _channel_dft_tables()
    bdwf = _block_diag(w_fourier)
    return pl.pallas_call(
        _prep_kernel,
        grid=(depth,),
        in_specs=[
            pl.BlockSpec((None, D_MODEL, d_in), lambda l: (l, 0, 0)),
            pl.BlockSpec((None, D_FOURIER, D_FOURIER), lambda l: (l, 0, 0)),
            _const_spec((D_FOURIER, D_FOURIER)),
            _const_spec((D_FOURIER, D_FOURIER)),
        ],
        out_specs=pl.BlockSpec((None, D_MODEL, n_out), lambda l: (l, 0, 0)),
        out_shape=jax.ShapeDtypeStruct((depth, D_MODEL, n_out), BF16),
        compiler_params=_params("parallel"),
        name="fold_in_weights",
    )(w_in, bdwf, bdc, bds)


def _rms_norm(x, g):
    ms = jnp.mean(x * x, axis=-1, keepdims=True)
    return x * lax.rsqrt(ms + EPS) * g


def _gelu_tanh(x):
    c = math.sqrt(2.0 / math.pi)
    return 0.5 * x * (1.0 + jnp.tanh(c * (x + 0.044715 * (x * x * x))))


def _mod_spec(layer, row0):
    return pl.BlockSpec((None, None, 1, N_MOD * D_MODEL), lambda b, i: (layer, row0 + b, 0, 0))


def _inproj_kernel(x_ref, mod_ref, g_ref, w_ref, ua_ref, uv_ref, ur_ref, gg_ref):
    shift = mod_ref[:, 0:D_MODEL]
    scale = mod_ref[:, D_MODEL:2 * D_MODEL]
    h = _rms_norm(x_ref[...], g_ref[...]) * (1.0 + scale) + shift
    z = jnp.dot(h.astype(BF16), w_ref[...], preferred_element_type=F32)
    ua_ref[...] = z[:, 0:D_FOURIER].astype(BF16)
    uv_ref[...] = z[:, D_FOURIER:2 * D_FOURIER].astype(BF16)
    r0 = 2 * D_FOURIER
    g0 = r0 + D_LRU
    for c in range(N_SLAB):
        ur_ref[c] = z[:, r0 + c * LANES:r0 + (c + 1) * LANES]
        gg_ref[c] = _gelu_tanh(z[:, g0 + c * LANES:g0 + (c + 1) * LANES])


def _in_projection(x, mod, layer, row0, g_mix, w_fold):
    b, s, _ = x.shape
    n_out = w_fold.shape[-1]
    slab = jax.ShapeDtypeStruct((b, N_SLAB, s, LANES), F32)
    half = jax.ShapeDtypeStruct((b, s, D_FOURIER), BF16)
    slab_spec = pl.BlockSpec((None, N_SLAB, ROW_TILE, LANES), lambda bi, i: (bi, 0, i, 0))
    half_spec = pl.BlockSpec((None, ROW_TILE, D_FOURIER), lambda bi, i: (bi, i, 0))
    return pl.pallas_call(
        _inproj_kernel,
        grid=(b, s // ROW_TILE),
        in_specs=[
            pl.BlockSpec((None, ROW_TILE, D_MODEL), lambda bi, i: (bi, i, 0)),
            _mod_spec(layer, row0),
            _const_spec((1, D_MODEL)),
            _const_spec((D_MODEL, n_out)),
        ],
        out_specs=[half_spec, half_spec, slab_spec, slab_spec],
        out_shape=[half, half, slab, slab],
        compiler_params=_params("parallel", "parallel"),
        name="inproj",
    )(x, mod, g_mix.reshape(1, D_MODEL), w_fold)


def _dft_tables(seq):
    n2 = DFT_N2
    n1 = seq // n2
    k2 = np.arange(n2)
    ang = 2.0 * np.pi * np.outer(k2, k2) / n2
    c2 = np.cos(ang) / math.sqrt(n2)
    s2 = np.sin(ang) / math.sqrt(n2)
    fa = np.concatenate([c2, -s2], axis=0)
    fv = np.concatenate([s2, c2], axis=0)
    g = DFT_GROUP
    n_groups = n2 // g
    rows = n1 * g
    k1 = np.arange(n1)
    s1 = np.arange(n1)
    kp = np.arange(g)
    scale = 1.0 / math.sqrt(n1 * HEAD_DIM)
    lc = np.zeros((n_groups, n1, g, n1, g), np.float64)
    ls = np.zeros((n_groups, n1, g, n1, g), np.float64)
    for grp in range(n_groups):
        k = n2 * k1[:, None] + (g * grp + kp)[None, :]
        ang_b = 2.0 * np.pi * (k[:, :, None] * s1[None, None, :] % seq) / seq
        for q in range(g):
            lc[grp, :, q, :, q] = np.cos(ang_b[:, q, :]) * scale
            ls[grp, :, q, :, q] = np.sin(ang_b[:, q, :]) * scale
    lc = lc.reshape(n_groups, rows, rows)
    ls = ls.reshape(n_groups, rows, rows)
    as_bf16 = lambda a: jnp.asarray(a.astype(np.float32)).astype(BF16)
    return as_bf16(fa), as_bf16(fv), as_bf16(lc), as_bf16(ls)


def _dft_a_kernel(a_ref, v_ref, fa_ref, fv_ref, tre_ref, tim_ref, *, n_cols):
    for j in range(n_cols):
        cols = slice(j * D_FOURIER, (j + 1) * D_FOURIER)
        o = (jnp.dot(fa_ref[...], a_ref[:, cols], preferred_element_type=F32)
             + jnp.dot(fv_ref[...], v_ref[:, cols], preferred_element_type=F32))
        tre_ref[j] = o[0:DFT_N2].astype(BF16)
        tim_ref[j] = o[DFT_N2:].astype(BF16)


def _dft_b_kernel(tr_ref, ti_ref, lc_ref, ls_ref, y_ref, *, n1):
    rows = n1 * DFT_GROUP
    tr = tr_ref[...].reshape(rows, D_FOURIER)
    ti = ti_ref[...].reshape(rows, D_FOURIER)
    o = (jnp.dot(lc_ref[...], tr, preferred_element_type=F32)
         + jnp.dot(ls_ref[...], ti, preferred_element_type=F32))
    y_ref[...] = o.astype(BF16).reshape(n1, DFT_GROUP, D_FOURIER)


def _fourier_mix(ua, uv):
    b, s, _ = ua.shape
    n2 = DFT_N2
    n1 = s // n2
    n_cols = min(4, n1)
    fa, fv, lc, ls = _dft_tables(s)
    a2 = ua.reshape(b, n2, n1 * D_FOURIER)
    v2 = uv.reshape(b, n2, n1 * D_FOURIER)
    t_shape = jax.ShapeDtypeStruct((b, n1, n2, D_FOURIER), BF16)
    in_spec = pl.BlockSpec((None, n2, n_cols * D_FOURIER), lambda bi, j: (bi, 0, j))
    t_spec = pl.BlockSpec((None, n_cols, n2, D_FOURIER), lambda bi, j: (bi, j, 0, 0))
    tre, tim = pl.pallas_call(
        functools.partial(_dft_a_kernel, n_cols=n_cols),
        grid=(b, n1 // n_cols),
        in_specs=[in_spec, in_spec, _const_spec((2 * n2, n2)), _const_spec((2 * n2, n2))],
        out_specs=[t_spec, t_spec],
        out_shape=[t_shape, t_shape],
        compiler_params=_params("parallel", "parallel"),
        name="dft_a",
    )(a2, v2, fa, fv)
    rows = n1 * DFT_GROUP
    g_spec = pl.BlockSpec((None, n1, DFT_GROUP, D_FOURIER), lambda bi, g: (bi, 0, g, 0))
    l_spec = pl.BlockSpec((None, rows, rows), lambda bi, g: (g, 0, 0))
    y = pl.pallas_call(
        functools.partial(_dft_b_kernel, n1=n1),
        grid=(b, n2 // DFT_GROUP),
        in_specs=[g_spec, g_spec, l_spec, l_spec],
        out_specs=g_spec,
        out_shape=jax.ShapeDtypeStruct((b, n1, n2, D_FOURIER), BF16),
        compiler_params=_params("parallel", "parallel"),
        name="dft_b",
    )(tre, tim, lc, ls)
    return y.reshape(b, s, D_FOURIER)


def _strided_rows(ref, t):
    return jnp.concatenate([ref[c, pl.ds(t, SUBLANES, stride=SEG), :] for c in range(N_SLAB)], axis=1)


def _single_row(ref, r):
    return jnp.concatenate([ref[c, r:r + 1, :] for c in range(N_SLAB)], axis=1)


def _lru_kernel(ur_ref, prev_ref, next_ref, gg_ref, cw_ref, cb_ref, wg_ref, bg_ref, lam_ref,
                yl_ref, pf_ref, pb_ref, ah_ref,
                uc_scr, pre_scr, af_scr, bf_scr, ab_scr, bb_scr, hf_scr):
    i = pl.program_id(1)
    n_tiles = pl.num_programs(1)
    sub = lax.broadcasted_iota(jnp.int32, (SUBLANES, D_LRU), 0)

    prev_rows = [jnp.where(i > 0, _single_row(prev_ref, SUBLANES - 2 + q), 0.0) for q in range(2)]
    next_row = jnp.where(i < n_tiles - 1, _single_row(next_ref, 0), 0.0)

    def halo_before(q):
        body = pltpu.roll(_strided_rows(ur_ref, SEG - 2 + q), 1, 0)
        return jnp.where(sub == 0, prev_rows[q], body)

    window = [halo_before(0), halo_before(1), _strided_rows(ur_ref, 0)]
    cw = [cw_ref[k:k + 1, :] for k in range(CONV_WIDTH)]
    cbias = cb_ref[...]
    for t in range(SEG):
        if t + 1 < SEG:
            nxt = _strided_rows(ur_ref, t + 1)
        else:
            nxt = jnp.where(sub == SUBLANES - 1, next_row, pltpu.roll(_strided_rows(ur_ref, 0), SUBLANES - 1, 0))
        window.append(nxt)
        uc = cbias + window[0] * cw[0] + window[1] * cw[1] + window[2] * cw[2] + window[3] * cw[3]
        uc_scr[t * SUBLANES:(t + 1) * SUBLANES, :] = uc
        window.pop(0)

    pre_scr[...] = jnp.dot(uc_scr[...].astype(BF16), wg_ref[...], preferred_element_type=F32) + bg_ref[...]

    neg_lam = -lam_ref[...]
    softplus = jnp.maximum(neg_lam, 0.0) + jnp.log1p(jnp.exp(-jnp.abs(neg_lam)))
    c_lam = -LRU_C * softplus
    blk = 64

    def gate_block(k, carry):
        rows = pl.ds(pl.multiple_of(k * blk, blk), blk)
        uc = uc_scr[rows, :]
        for d, (a_scr, b_scr) in enumerate(((af_scr, bf_scr), (ab_scr, bb_scr))):
            r = jax.nn.sigmoid(pre_scr[rows, pl.ds(d * 2 * D_LRU, D_LRU)])
            ig = jax.nn.sigmoid(pre_scr[rows, pl.ds(d * 2 * D_LRU + D_LRU, D_LRU)])
            a = jnp.exp(r * c_lam[d:d + 1, :])
            mult = jnp.sqrt(jnp.maximum(1.0 - a * a, 0.0))
            a_scr[rows, :] = a
            b_scr[rows, :] = mult * (ig * uc)
        return carry

    lax.fori_loop(0, ROW_TILE // blk, gate_block, 0)

    def store_strided(ref, t, val):
        for c in range(N_SLAB):
            ref[c, pl.ds(t, SUBLANES, stride=SEG), :] = val[:, c * LANES:(c + 1) * LANES]

    zeros = jnp.zeros((SUBLANES, D_LRU), F32)
    ones = jnp.ones((SUBLANES, D_LRU), F32)

    def fwd_step(t, carry):
        h, p = carry
        rows = pl.ds(pl.multiple_of(t * SUBLANES, SUBLANES), SUBLANES)
        a = af_scr[rows, :]
        h = a * h + bf_scr[rows, :]
        p = a * p
        hf_scr[rows, :] = h
        store_strided(pf_ref, t, p * _strided_rows(gg_ref, t))
        return h, p

    h_f, p_f = lax.fori_loop(0, SEG, fwd_step, (zeros, ones))

    def bwd_step(n, carry):
        h, p = carry
        t = SEG - 1 - n
        rows = pl.ds(pl.multiple_of(t * SUBLANES, SUBLANES), SUBLANES)
        a = ab_scr[rows, :]
        h = a * h + bb_scr[rows, :]
        p = a * p
        g = _strided_rows(gg_ref, t)
        store_strided(yl_ref, t, (hf_scr[rows, :] + h) * g)
        store_strided(pb_ref, t, p * g)
        return h, p

    h_b, p_b = lax.fori_loop(0, SEG, bwd_step, (zeros, ones))
    ah_ref[0] = p_f
    ah_ref[1] = h_f
    ah_ref[2] = p_b
    ah_ref[3] = h_b


def _lru_local(ur, gg, conv_w, conv_b, wg, bg, lam):
    b, _, s, _ = ur.shape
    n_tiles = s // ROW_TILE
    blocks_per_tile = ROW_TILE // SUBLANES
    last_block = s // SUBLANES - 1
    slab = jax.ShapeDtypeStruct((b, N_SLAB, s, LANES), F32)
    slab_spec = pl.BlockSpec((None, N_SLAB, ROW_TILE, LANES), lambda bi, i: (bi, 0, i, 0))
    prev_spec = pl.BlockSpec((None, N_SLAB, SUBLANES, LANES),
                             lambda bi, i: (bi, 0, jnp.maximum(i * blocks_per_tile - 1, 0), 0))
    next_spec = pl.BlockSpec((None, N_SLAB, SUBLANES, LANES),
                             lambda bi, i: (bi, 0, jnp.minimum((i + 1) * blocks_per_tile, last_block), 0))
    n_seg = s // SEG
    tile_f32 = pltpu.VMEM((ROW_TILE, D_LRU), F32)
    return pl.pallas_call(
        _lru_kernel,
        grid=(b, n_tiles),
        in_specs=[
            slab_spec, prev_spec, next_spec, slab_spec,
            _const_spec((CONV_WIDTH, D_LRU)),
            _const_spec((1, D_LRU)),
            _const_spec((D_LRU, 4 * D_LRU)),
            _const_spec((1, 4 * D_LRU)),
            _const_spec((2, D_LRU)),
        ],
        out_specs=[slab_spec, slab_spec, slab_spec,
                   pl.BlockSpec((None, 4, SUBLANES, D_LRU), lambda bi, i: (bi, 0, i, 0))],
        out_shape=[slab, slab, slab, jax.ShapeDtypeStruct((b, 4, n_seg, D_LRU), F32)],
        scratch_shapes=[tile_f32, pltpu.VMEM((ROW_TILE, 4 * D_LRU), F32),
                        tile_f32, tile_f32, tile_f32, tile_f32, tile_f32],
        compiler_params=_params("parallel", "parallel"),
        name="lru_local",
    )(ur, ur, ur, gg, conv_w, conv_b.reshape(1, D_LRU), wg, bg, lam)


def _carry_kernel(ah_ref, cf_ref, cb_ref, *, n_seg):
    def fwd(sg, c):
        cf_ref[pl.ds(sg, 1), :] = c
        return ah_ref[1, pl.ds(sg, 1), :] + ah_ref[0, pl.ds(sg, 1), :] * c

    def bwd(n, c):
        sg = n_seg - 1 - n
        cb_ref[pl.ds(sg, 1), :] = c
        return ah_ref[3, pl.ds(sg, 1), :] + ah_ref[2, pl.ds(sg, 1), :] * c

    zero = jnp.zeros((1, D_LRU), F32)
    lax.fori_loop(0, n_seg, fwd, zero)
    lax.fori_loop(0, n_seg, bwd, zero)


def _segment_carries(ah):
    b, _, n_seg, _ = ah.shape
    c_shape = jax.ShapeDtypeStruct((b, n_seg, D_LRU), F32)
    c_spec = pl.BlockSpec((None, n_seg, D_LRU), lambda bi: (bi, 0, 0))
    return pl.pallas_call(
        functools.partial(_carry_kernel, n_seg=n_seg),
        grid=(b,),
        in_specs=[pl.BlockSpec((None, 4, n_seg, D_LRU), lambda bi: (bi, 0, 0, 0))],
        out_specs=[c_spec, c_spec],
        out_shape=[c_shape, c_shape],
        compiler_params=_params("parallel"),
        name="lru_carry",
    )(ah)


def _ffn_kernel(x_ref, yf_ref, yl_ref, pf_ref, pb_ref, cf_ref, cb_ref, mod_ref, g_ref,
                wo_ref, w1_ref, w2_ref, gfin_ref, o_ref, yr_scr, *, final_norm):
    for c in range(N_SLAB):
        lanes = slice(c * LANES, (c + 1) * LANES)
        for j in range(ROW_TILE // SEG):
            rows = slice(j * SEG, (j + 1) * SEG)
            yr = (yl_ref[c, rows, :] + pf_ref[c, rows, :] * cf_ref[j:j + 1, lanes]
                  + pb_ref[c, rows, :] * cb_ref[j:j + 1, lanes])
            yr_scr[rows, lanes] = yr.astype(BF16)
    gate1 = mod_ref[:, 2 * D_MODEL:3 * D_MODEL]
    shift2 = mod_ref[:, 3 * D_MODEL:4 * D_MODEL]
    scale2 = mod_ref[:, 4 * D_MODEL:5 * D_MODEL]
    gate2 = mod_ref[:, 5 * D_MODEL:6 * D_MODEL]
    mix = (jnp.dot(yf_ref[...], wo_ref[0:D_FOURIER, :], preferred_element_type=F32)
           + jnp.dot(yr_scr[...], wo_ref[D_FOURIER:, :], preferred_element_type=F32))
    x1 = x_ref[...] + gate1 * mix
    h = (_rms_norm(x1, g_ref[...]) * (1.0 + scale2) + shift2).astype(BF16)
    acc = jnp.zeros((ROW_TILE, D_MODEL), F32)
    for k in range(D_FF // FF_CHUNK):
        cols = slice(k * FF_CHUNK, (k + 1) * FF_CHUNK)
        t = jnp.maximum(jnp.dot(h, w1_ref[:, cols], preferred_element_type=F32), 0.0)
        acc = acc + jnp.dot((t * t).astype(BF16), w2_ref[cols, :], preferred_element_type=F32)
    x2 = x1 + gate2 * acc
    if final_norm:
        x2 = _rms_norm(x2, gfin_ref[...])
    o_ref[...] = x2


def _out_and_mlp(x, yf, yl, pf, pb, cf, cb, mod, layer, row0, g_mlp, w_out, w_ff1, w_ff2, g_final,
                 final_norm):
    b, s, _ = x.shape
    segs = ROW_TILE // SEG
    x_spec = pl.BlockSpec((None, ROW_TILE, D_MODEL), lambda bi, i: (bi, i, 0))
    slab_spec = pl.BlockSpec((None, N_SLAB, ROW_TILE, LANES), lambda bi, i: (bi, 0, i, 0))
    c_spec = pl.BlockSpec((None, segs, D_LRU), lambda bi, i: (bi, i, 0))
    return pl.pallas_call(
        functools.partial(_ffn_kernel, final_norm=final_norm),
        grid=(b, s // ROW_TILE),
        in_specs=[
            x_spec,
            pl.BlockSpec((None, ROW_TILE, D_FOURIER), lambda bi, i: (bi, i, 0)),
            slab_spec, slab_spec, slab_spec, c_spec, c_spec,
            _mod_spec(layer, row0),
            _const_spec((1, D_MODEL)),
            _const_spec((D_MODEL, D_MODEL)),
            _const_spec((D_MODEL, D_FF)),
            _const_spec((D_FF, D_MODEL)),
            _const_spec((1, D_MODEL)),
        ],
        out_specs=x_spec,
        out_shape=jax.ShapeDtypeStruct(x.shape, F32),
        scratch_shapes=[pltpu.VMEM((ROW_TILE, D_LRU), BF16)],
        compiler_params=_params("parallel", "parallel"),
        name="outproj_mlp",
    )(x, yf, yl, pf, pb, cf, cb, mod, g_mlp.reshape(1, D_MODEL), w_out, w_ff1, w_ff2,
      g_final.reshape(1, D_MODEL))


def _trunk(x, mod, row0, layers, g_final):
    depth = len(layers)
    for l, p in enumerate(layers):
        ua, uv, ur, gg = _in_projection(x, mod, l, row0, p["g_mix"], p["w_fold"])
        yf = _fourier_mix(ua, uv)
        yl, pf, pb, ah = _lru_local(ur, gg, p["conv_w"], p["conv_b"], p["wg"], p["bg"], p["lam"])
        cf, cb = _segment_carries(ah)
        x = _out_and_mlp(x, yf, yl, pf, pb, cf, cb, mod, l, row0, p["g_mlp"], p["w_out"],
                         p["w_ff1"], p["w_ff2"], g_final, final_norm=(l == depth - 1))
    return x


def kernel(x_prompt, x_sample, c_prompt, c_sample, g_mix, g_mlp, w_mod, b_mod, w_in, w_fourier,
           conv_w, conv_b, w_rgate, b_rgate, w_igate, b_igate, lru_lambda, w_out, w_ff1, w_ff2,
           g_final):
    depth = w_in.shape[0]
    n_prompt = c_prompt.shape[0]
    n_cond = n_prompt + c_sample.shape[0]
    pad = -n_cond % SUBLANES
    c_all = jnp.concatenate([c_prompt, c_sample, jnp.zeros((pad, D_MODEL), F32)], axis=0)
    mod = _modulation(c_all, w_mod, b_mod)
    mod = mod.reshape(depth, n_cond + pad, 1, N_MOD * D_MODEL)

    w_fold = _fold_in_weights(w_in, w_fourier)
    wg = jnp.concatenate([_block_diag(w_rgate[:, 0]), _block_diag(w_igate[:, 0]),
                          _block_diag(w_rgate[:, 1]), _block_diag(w_igate[:, 1])], axis=-1).astype(BF16)
    bg = jnp.concatenate([b_rgate[:, 0].reshape(depth, 1, D_LRU), b_igate[:, 0].reshape(depth, 1, D_LRU),
                          b_rgate[:, 1].reshape(depth, 1, D_LRU), b_igate[:, 1].reshape(depth, 1, D_LRU)],
                         axis=-1)
    w_out_b = w_out.astype(BF16)
    w_ff1_b = w_ff1.astype(BF16)
    w_ff2_b = w_ff2.astype(BF16)
    layers = [dict(g_mix=g_mix[l], g_mlp=g_mlp[l], w_fold=w_fold[l], conv_w=conv_w[l], conv_b=conv_b[l],
                   wg=wg[l], bg=bg[l], lam=lru_lambda[l], w_out=w_out_b[l], w_ff1=w_ff1_b[l],
                   w_ff2=w_ff2_b[l]) for l in range(depth)]
    y_prompt = _trunk(x_prompt, mod, 0, layers, g_final)
    y_sample = _trunk(x_sample, mod, n_prompt, layers, g_final)
    return (y_prompt, y_sample)
```

```python
import functools
import math

import numpy as np
import jax
import jax.numpy as jnp
from jax import lax
from jax.experimental import pallas as pl
from jax.experimental.pallas import tpu as pltpu

F32 = jnp.float32
BF16 = jnp.bfloat16
HIGHEST = lax.Precision.HIGHEST

D_MODEL = 1024
HEAD_DIM = 64
D_FOURIER = 512
D_LRU = 512
N_HEADS = 8
D_FF = 4096
N_MOD = 6
EPS = 1e-6
LRU_C = 8.0
CONV_WIDTH = 4

LANES = 128
SUBLANES = 8
BF16_ROWS = 16
N_SLAB = D_LRU // LANES
ROW_TILE = 512
SEG = ROW_TILE // SUBLANES
SEG_PITCH = SEG + SUBLANES
TILE_PITCH = SUBLANES * SEG_PITCH
LOG2E = 1.4426950408889634
MXU_K = 256
GATE_COLS = 4 * MXU_K
SCAN_UNROLL = 2
DFT_N2 = 256
DFT_GROUP = BF16_ROWS
FF_CHUNK = 1024
MOD_COLS = 1536
VMEM_LIMIT = 56 * 1024 * 1024


def _const_spec(shape):
    n = len(shape)
    return pl.BlockSpec(shape, lambda *_: (0,) * n, pipeline_mode=pl.Buffered(1))


def _params(*sem):
    return pltpu.CompilerParams(dimension_semantics=sem, vmem_limit_bytes=VMEM_LIMIT)


def _mod_kernel(c_ref, w_ref, b_ref, o_ref):
    c = c_ref[...]
    s = c * jax.nn.sigmoid(c)
    o_ref[...] = jnp.dot(s, w_ref[...], precision=HIGHEST, preferred_element_type=F32) + b_ref[...]


def _modulation(c_all, w_mod, b_mod):
    depth = w_mod.shape[0]
    n_out = N_MOD * D_MODEL
    rows = c_all.shape[0]
    return pl.pallas_call(
        _mod_kernel,
        grid=(depth, n_out // MOD_COLS),
        in_specs=[
            pl.BlockSpec((rows, D_MODEL), lambda l, j: (0, 0)),
            pl.BlockSpec((None, D_MODEL, MOD_COLS), lambda l, j: (l, 0, j)),
            pl.BlockSpec((None, 1, MOD_COLS), lambda l, j: (l, 0, j)),
        ],
        out_specs=pl.BlockSpec((None, rows, MOD_COLS), lambda l, j: (l, 0, j)),
        out_shape=jax.ShapeDtypeStruct((depth, rows, n_out), F32),
        compiler_params=_params("parallel", "parallel"),
        name="adaln_mod",
    )(c_all, w_mod, b_mod.reshape(depth, 1, n_out))


def _prep_kernel(win_ref, bdwf_ref, bdc_ref, bds_ref, o_ref):
    wf = bdwf_ref[...]
    ma = jnp.dot(bdc_ref[...], wf, precision=HIGHEST, preferred_element_type=F32)
    mv = -jnp.dot(bds_ref[...], wf, precision=HIGHEST, preferred_element_type=F32)
    wi = win_ref[:, 0:D_FOURIER]
    o_ref[:, 0:D_FOURIER] = jnp.dot(wi, ma, precision=HIGHEST, preferred_element_type=F32).astype(BF16)
    o_ref[:, D_FOURIER:2 * D_FOURIER] = jnp.dot(
        wi, mv, precision=HIGHEST, preferred_element_type=F32).astype(BF16)
    o_ref[:, 2 * D_FOURIER:] = win_ref[:, D_FOURIER:].astype(BF16)


def _block_diag(w):
    h, d, e = w.shape[-3:]
    eye = jnp.eye(h, dtype=w.dtype)
    out = w[..., :, :, None, :] * eye[:, None, :, None]
    return out.reshape(w.shape[:-3] + (h * d, h * e))


def _channel_dft_tables():
    k = np.arange(HEAD_DIM)
    ang = 2.0 * np.pi * np.outer(k, k) / HEAD_DIM
    eye = np.eye(N_HEADS)
    return (jnp.asarray(np.kron(eye, np.cos(ang)), F32), jnp.asarray(np.kron(eye, np.sin(ang)), F32))


def _fold_in_weights(w_in, w_fourier):
    depth = w_in.shape[0]
    d_in = w_in.shape[2]
    n_out = d_in + D_FOURIER
    bdc, bds = _channel_dft_tables()
    bdwf = _block_diag(w_fourier)
    return pl.pallas_call(
        _prep_kernel,
        grid=(depth,),
        in_specs=[
            pl.BlockSpec((None, D_MODEL, d_in), lambda l: (l, 0, 0)),
            pl.BlockSpec((None, D_FOURIER, D_FOURIER), lambda l: (l, 0, 0)),
            _const_spec((D_FOURIER, D_FOURIER)),
            _const_spec((D_FOURIER, D_FOURIER)),
        ],
        out_specs=pl.BlockSpec((None, D_MODEL, n_out), lambda l: (l, 0, 0)),
        out_shape=jax.ShapeDtypeStruct((depth, D_MODEL, n_out), BF16),
        compiler_params=_params("parallel"),
        name="fold_in_weights",
    )(w_in, bdwf, bdc, bds)


def _rms_norm(x, g):
    ms = jnp.mean(x * x, axis=-1, keepdims=True)
    return x * lax.rsqrt(ms + EPS) * g


def _gelu_tanh(x):
    c = math.sqrt(2.0 / math.pi)
    return 0.5 * x * (1.0 + jnp.tanh(c * (x + 0.044715 * (x * x * x))))


def _mod_spec(layer, row0):
    return pl.BlockSpec((None, None, 1, N_MOD * D_MODEL), lambda b, i: (layer, row0 + b, 0, 0))


def _inproj_kernel(x_ref, mod_ref, g_ref, w_ref, ua_ref, uv_ref, ur_ref, gg_ref):
    shift = mod_ref[:, 0:D_MODEL]
    scale = mod_ref[:, D_MODEL:2 * D_MODEL]
    h = _rms_norm(x_ref[...], g_ref[...]) * (1.0 + scale) + shift
    z = jnp.dot(h.astype(BF16), w_ref[...], preferred_element_type=F32)
    ua_ref[...] = z[:, 0:D_FOURIER].astype(BF16)
    uv_ref[...] = z[:, D_FOURIER:2 * D_FOURIER].astype(BF16)
    r0 = 2 * D_FOURIER
    g0 = r0 + D_LRU
    pad = jnp.zeros((SEG_PITCH - SEG, LANES), F32)
    for c in range(N_SLAB):
        ur = z[:, r0 + c * LANES:r0 + (c + 1) * LANES]
        gg = _gelu_tanh(z[:, g0 + c * LANES:g0 + (c + 1) * LANES])
        for j in range(SUBLANES):
            ur_ref[c, j * SEG_PITCH:j * SEG_PITCH + SEG, :] = ur[j * SEG:(j + 1) * SEG]
            gg_ref[c, j * SEG_PITCH:j * SEG_PITCH + SEG, :] = gg[j * SEG:(j + 1) * SEG]
            ur_ref[c, j * SEG_PITCH + SEG:(j + 1) * SEG_PITCH, :] = pad
            gg_ref[c, j * SEG_PITCH + SEG:(j + 1) * SEG_PITCH, :] = pad


def _slab_shape(b, s):
    return jax.ShapeDtypeStruct((b, N_SLAB, s // ROW_TILE * TILE_PITCH, LANES), F32)


def _slab_spec():
    return pl.BlockSpec((None, N_SLAB, TILE_PITCH, LANES), lambda bi, i: (bi, 0, i, 0))


def _in_projection(x, mod, layer, row0, g_mix, w_fold):
    b, s, _ = x.shape
    n_out = w_fold.shape[-1]
    slab = _slab_shape(b, s)
    half = jax.ShapeDtypeStruct((b, s, D_FOURIER), BF16)
    slab_spec = _slab_spec()
    half_spec = pl.BlockSpec((None, ROW_TILE, D_FOURIER), lambda bi, i: (bi, i, 0))
    return pl.pallas_call(
        _inproj_kernel,
        grid=(b, s // ROW_TILE),
        in_specs=[
            pl.BlockSpec((None, ROW_TILE, D_MODEL), lambda bi, i: (bi, i, 0)),
            _mod_spec(layer, row0),
            _const_spec((1, D_MODEL)),
            _const_spec((D_MODEL, n_out)),
        ],
        out_specs=[half_spec, half_spec, slab_spec, slab_spec],
        out_shape=[half, half, slab, slab],
        compiler_params=_params("parallel", "parallel"),
        name="inproj",
    )(x, mod, g_mix.reshape(1, D_MODEL), w_fold)


def _dft_tables(seq):
    n2 = DFT_N2
    n1 = seq // n2
    k2 = np.arange(n2)
    ang = 2.0 * np.pi * np.outer(k2, k2) / n2
    c2 = np.cos(ang) / math.sqrt(n2)
    s2 = np.sin(ang) / math.sqrt(n2)
    fa = np.concatenate([c2, -s2], axis=0)
    fv = np.concatenate([s2, c2], axis=0)
    g = DFT_GROUP
    n_groups = n2 // g
    rows = n1 * g
    k1 = np.arange(n1)
    s1 = np.arange(n1)
    kp = np.arange(g)
    scale = 1.0 / math.sqrt(n1 * HEAD_DIM)
    lc = np.zeros((n_groups, n1, g, n1, g), np.float64)
    ls = np.zeros((n_groups, n1, g, n1, g), np.float64)
    for grp in range(n_groups):
        k = n2 * k1[:, None] + (g * grp + kp)[None, :]
        ang_b = 2.0 * np.pi * (k[:, :, None] * s1[None, None, :] % seq) / seq
        for q in range(g):
            lc[grp, :, q, :, q] = np.cos(ang_b[:, q, :]) * scale
            ls[grp, :, q, :, q] = np.sin(ang_b[:, q, :]) * scale
    lc = lc.reshape(n_groups, rows, rows)
    ls = ls.reshape(n_groups, rows, rows)
    as_bf16 = lambda a: jnp.asarray(a.astype(np.float32)).astype(BF16)
    return as_bf16(fa), as_bf16(fv), as_bf16(lc), as_bf16(ls)


def _dft_a_kernel(a_ref, v_ref, fa_ref, fv_ref, tre_ref, tim_ref, *, n_cols):
    for j in range(n_cols):
        cols = slice(j * D_FOURIER, (j + 1) * D_FOURIER)
        o = (jnp.dot(fa_ref[...], a_ref[:, cols], preferred_element_type=F32)
             + jnp.dot(fv_ref[...], v_ref[:, cols], preferred_element_type=F32))
        tre_ref[j] = o[0:DFT_N2].astype(BF16)
        tim_ref[j] = o[DFT_N2:].astype(BF16)


def _dft_b_kernel(tr_ref, ti_ref, lc_ref, ls_ref, y_ref, *, n1):
    rows = n1 * DFT_GROUP
    tr = tr_ref[...].reshape(rows, D_FOURIER)
    ti = ti_ref[...].reshape(rows, D_FOURIER)
    o = (jnp.dot(lc_ref[...], tr, preferred_element_type=F32)
         + jnp.dot(ls_ref[...], ti, preferred_element_type=F32))
    y_ref[...] = o.astype(BF16).reshape(n1, DFT_GROUP, D_FOURIER)


def _fourier_mix(ua, uv):
    b, s, _ = ua.shape
    n2 = DFT_N2
    n1 = s // n2
    n_cols = min(4, n1)
    fa, fv, lc, ls = _dft_tables(s)
    a2 = ua.reshape(b, n2, n1 * D_FOURIER)
    v2 = uv.reshape(b, n2, n1 * D_FOURIER)
    t_shape = jax.ShapeDtypeStruct((b, n1, n2, D_FOURIER), BF16)
    in_spec = pl.BlockSpec((None, n2, n_cols * D_FOURIER), lambda bi, j: (bi, 0, j))
    t_spec = pl.BlockSpec((None, n_cols, n2, D_FOURIER), lambda bi, j: (bi, j, 0, 0))
    tre, tim = pl.pallas_call(
        functools.partial(_dft_a_kernel, n_cols=n_cols),
        grid=(b, n1 // n_cols),
        in_specs=[in_spec, in_spec, _const_spec((2 * n2, n2)), _const_spec((2 * n2, n2))],
        out_specs=[t_spec, t_spec],
        out_shape=[t_shape, t_shape],
        compiler_params=_params("parallel", "parallel"),
        name="dft_a",
    )(a2, v2, fa, fv)
    rows = n1 * DFT_GROUP
    g_spec = pl.BlockSpec((None, n1, DFT_GROUP, D_FOURIER), lambda bi, g: (bi, 0, g, 0))
    l_spec = pl.BlockSpec((None, rows, rows), lambda bi, g: (g, 0, 0))
    y = pl.pallas_call(
        functools.partial(_dft_b_kernel, n1=n1),
        grid=(b, n2 // DFT_GROUP),
        in_specs=[g_spec, g_spec, l_spec, l_spec],
        out_specs=g_spec,
        out_shape=jax.ShapeDtypeStruct((b, n1, n2, D_FOURIER), BF16),
        compiler_params=_params("parallel", "parallel"),
        name="dft_b",
    )(tre, tim, lc, ls)
    return y.reshape(b, s, D_FOURIER)


def _segment_rows(ref, c, t):
    return ref[c, pl.ds(t, SUBLANES, stride=SEG_PITCH), :]


def _lru_kernel(ur_ref, prev_ref, next_ref, gg_ref, cw_ref, cb_ref, wg_ref, bg_ref, lam_ref,
                yl_ref, pf_ref, pb_ref, ah_ref, uc_scr, pre_scr, hf_scr):
    i = pl.program_id(1)
    n_tiles = pl.num_programs(1)
    sub = lax.broadcasted_iota(jnp.int32, (SUBLANES, LANES), 0)
    slab_lanes = [slice(c * LANES, (c + 1) * LANES) for c in range(N_SLAB)]

    for c in range(N_SLAB):
        lanes = slab_lanes[c]
        prev_rows = [jnp.where(i > 0, prev_ref[c, SUBLANES - 2 + q:SUBLANES - 1 + q, :], 0.0) for q in range(2)]
        next_row = jnp.where(i < n_tiles - 1, next_ref[c, 0:1, :], 0.0)
        first = _segment_rows(ur_ref, c, 0)
        window = [jnp.where(sub == 0, prev_rows[q], pltpu.roll(_segment_rows(ur_ref, c, SEG - 2 + q), 1, 0))
                  for q in range(2)] + [first]
        cw = [cw_ref[k:k + 1, lanes] for k in range(CONV_WIDTH)]
        cbias = cb_ref[:, lanes]
        for t in range(SEG):
            if t + 1 < SEG:
                nxt = _segment_rows(ur_ref, c, t + 1)
            else:
                nxt = jnp.where(sub == SUBLANES - 1, next_row, pltpu.roll(first, SUBLANES - 1, 0))
            window.append(nxt)
            uc = cbias + window[0] * cw[0] + window[1] * cw[1] + window[2] * cw[2] + window[3] * cw[3]
            uc_scr[t * SUBLANES:(t + 1) * SUBLANES, lanes] = uc
            window.pop(0)

    for hh in range(D_LRU // MXU_K):
        cols = slice(hh * GATE_COLS, (hh + 1) * GATE_COLS)
        lhs = uc_scr[:, hh * MXU_K:(hh + 1) * MXU_K].astype(BF16)
        pre_scr[:, cols] = jnp.dot(lhs, wg_ref[hh], preferred_element_type=F32) + bg_ref[:, cols]

    neg_lam = -lam_ref[...]
    softplus = jnp.maximum(neg_lam, 0.0) + jnp.log1p(jnp.exp(-jnp.abs(neg_lam)))
    k_all = (-0.5 * LRU_C * LOG2E) * softplus
    k_rows = [[jnp.broadcast_to(k_all[d:d + 1, slab_lanes[c]], (SUBLANES, LANES)) for c in range(N_SLAB)]
              for d in range(2)]

    def scan_inputs(rows, c, d):
        col = (c * LANES // MXU_K) * GATE_COLS + d * 2 * MXU_K + (c * LANES) % MXU_K
        th_r = jnp.tanh(pre_scr[rows, col:col + LANES])
        th_i = jnp.tanh(pre_scr[rows, col + MXU_K:col + MXU_K + LANES])
        k = k_rows[d][c]
        a = jnp.exp2(k * th_r + k)
        mult = jnp.sqrt(jnp.maximum(1.0 - a * a, 0.0))
        return a, mult * ((0.5 * th_i + 0.5) * uc_scr[rows, slab_lanes[c]])

    zeros = tuple(jnp.zeros((SUBLANES, LANES), F32) for _ in range(N_SLAB))
    ones = tuple(jnp.ones((SUBLANES, LANES), F32) for _ in range(N_SLAB))

    def fwd_step(t, carry):
        hs, ps = carry
        rows = pl.ds(pl.multiple_of(t * SUBLANES, SUBLANES), SUBLANES)
        new_h, new_p = [], []
        for c in range(N_SLAB):
            a, b = scan_inputs(rows, c, 0)
            h = a * hs[c] + b
            p = a * ps[c]
            hf_scr[rows, slab_lanes[c]] = h
            pf_ref[c, pl.ds(t, SUBLANES, stride=SEG_PITCH), :] = p * _segment_rows(gg_ref, c, t)
            new_h.append(h)
            new_p.append(p)
        return tuple(new_h), tuple(new_p)

    h_f, p_f = lax.fori_loop(0, SEG, fwd_step, (zeros, ones), unroll=SCAN_UNROLL)

    def bwd_step(n, carry):
        hs, ps = carry
        t = SEG - 1 - n
        rows = pl.ds(pl.multiple_of(t * SUBLANES, SUBLANES), SUBLANES)
        new_h, new_p = [], []
        for c in range(N_SLAB):
            a, b = scan_inputs(rows, c, 1)
            h = a * hs[c] + b
            p = a * ps[c]
            g = _segment_rows(gg_ref, c, t)
            yl_ref[c, pl.ds(t, SUBLANES, stride=SEG_PITCH), :] = (hf_scr[rows, slab_lanes[c]] + h) * g
            pb_ref[c, pl.ds(t, SUBLANES, stride=SEG_PITCH), :] = p * g
            new_h.append(h)
            new_p.append(p)
        return tuple(new_h), tuple(new_p)

    h_b, p_b = lax.fori_loop(0, SEG, bwd_step, (zeros, ones), unroll=SCAN_UNROLL)

    pad = jnp.zeros((SEG_PITCH - SEG, LANES), F32)
    for c in range(N_SLAB):
        ah_ref[0, :, slab_lanes[c]] = p_f[c]
        ah_ref[1, :, slab_lanes[c]] = h_f[c]
        ah_ref[2, :, slab_lanes[c]] = p_b[c]
        ah_ref[3, :, slab_lanes[c]] = h_b[c]
        for j in range(SUBLANES):
            pad_rows = slice(j * SEG_PITCH + SEG, (j + 1) * SEG_PITCH)
            yl_ref[c, pad_rows, :] = pad
            pf_ref[c, pad_rows, :] = pad
            pb_ref[c, pad_rows, :] = pad


def _lru_local(ur, gg, conv_w, conv_b, wg, bg, lam):
    b, _, rows, _ = ur.shape
    n_tiles = rows // TILE_PITCH
    blocks_per_tile = TILE_PITCH // SUBLANES
    last_block = rows // SUBLANES - 1
    s = n_tiles * ROW_TILE
    slab = _slab_shape(b, s)
    slab_spec = _slab_spec()
    prev_spec = pl.BlockSpec((None, N_SLAB, SUBLANES, LANES),
                             lambda bi, i: (bi, 0, jnp.maximum(i * blocks_per_tile - 2, 0), 0))
    next_spec = pl.BlockSpec((None, N_SLAB, SUBLANES, LANES),
                             lambda bi, i: (bi, 0, jnp.minimum((i + 1) * blocks_per_tile, last_block), 0))
    n_seg = s // SEG
    n_half = D_LRU // MXU_K
    tile_f32 = pltpu.VMEM((ROW_TILE, D_LRU), F32)
    return pl.pallas_call(
        _lru_kernel,
        grid=(b, n_tiles),
        in_specs=[
            slab_spec, prev_spec, next_spec, slab_spec,
            _const_spec((CONV_WIDTH, D_LRU)),
            _const_spec((1, D_LRU)),
            _const_spec((n_half, MXU_K, GATE_COLS)),
            _const_spec((1, n_half * GATE_COLS)),
            _const_spec((2, D_LRU)),
        ],
        out_specs=[slab_spec, slab_spec, slab_spec,
                   pl.BlockSpec((None, 4, SUBLANES, D_LRU), lambda bi, i: (bi, 0, i, 0))],
        out_shape=[slab, slab, slab, jax.ShapeDtypeStruct((b, 4, n_seg, D_LRU), F32)],
        scratch_shapes=[tile_f32, pltpu.VMEM((ROW_TILE, n_half * GATE_COLS), F32), tile_f32],
        compiler_params=_params("parallel", "parallel"),
        name="lru_local",
    )(ur, ur, ur, gg, conv_w, conv_b.reshape(1, D_LRU), wg, bg, lam)


def _carry_kernel(ah_ref, cf_ref, cb_ref, *, n_seg):
    def fwd(sg, c):
        cf_ref[pl.ds(sg, 1), :] = c
        return ah_ref[1, pl.ds(sg, 1), :] + ah_ref[0, pl.ds(sg, 1), :] * c

    def bwd(n, c):
        sg = n_seg - 1 - n
        cb_ref[pl.ds(sg, 1), :] = c
        return ah_ref[3, pl.ds(sg, 1), :] + ah_ref[2, pl.ds(sg, 1), :] * c

    zero = jnp.zeros((1, D_LRU), F32)
    lax.fori_loop(0, n_seg, fwd, zero)
    lax.fori_loop(0, n_seg, bwd, zero)


def _segment_carries(ah):
    b, _, n_seg, _ = ah.shape
    c_shape = jax.ShapeDtypeStruct((b, n_seg, D_LRU), F32)
    c_spec = pl.BlockSpec((None, n_seg, D_LRU), lambda bi: (bi, 0, 0))
    return pl.pallas_call(
        functools.partial(_carry_kernel, n_seg=n_seg),
        grid=(b,),
        in_specs=[pl.BlockSpec((None, 4, n_seg, D_LRU), lambda bi: (bi, 0, 0, 0))],
        out_specs=[c_spec, c_spec],
        out_shape=[c_shape, c_shape],
        compiler_params=_params("parallel"),
        name="lru_carry",
    )(ah)


def _ffn_kernel(x_ref, yf_ref, yl_ref, pf_ref, pb_ref, cf_ref, cb_ref, mod_ref, g_ref,
                wo_ref, w1_ref, w2_ref, gfin_ref, o_ref, yr_scr, *, final_norm):
    for c in range(N_SLAB):
        lanes = slice(c * LANES, (c + 1) * LANES)
        for j in range(ROW_TILE // SEG):
            rows = slice(j * SEG_PITCH, j * SEG_PITCH + SEG)
            yr = (yl_ref[c, rows, :] + pf_ref[c, rows, :] * cf_ref[j:j + 1, lanes]
                  + pb_ref[c, rows, :] * cb_ref[j:j + 1, lanes])
            yr_scr[j * SEG:(j + 1) * SEG, lanes] = yr.astype(BF16)
    gate1 = mod_ref[:, 2 * D_MODEL:3 * D_MODEL]
    shift2 = mod_ref[:, 3 * D_MODEL:4 * D_MODEL]
    scale2 = mod_ref[:, 4 * D_MODEL:5 * D_MODEL]
    gate2 = mod_ref[:, 5 * D_MODEL:6 * D_MODEL]
    mix = (jnp.dot(yf_ref[...], wo_ref[0:D_FOURIER, :], preferred_element_type=F32)
           + jnp.dot(yr_scr[...], wo_ref[D_FOURIER:, :], preferred_element_type=F32))
    x1 = x_ref[...] + gate1 * mix
    h = (_rms_norm(x1, g_ref[...]) * (1.0 + scale2) + shift2).astype(BF16)
    acc = jnp.zeros((ROW_TILE, D_MODEL), F32)
    for k in range(D_FF // FF_CHUNK):
        cols = slice(k * FF_CHUNK, (k + 1) * FF_CHUNK)
        t = jnp.maximum(jnp.dot(h, w1_ref[:, cols], preferred_element_type=F32), 0.0)
        acc = acc + jnp.dot((t * t).astype(BF16), w2_ref[cols, :], preferred_element_type=F32)
    x2 = x1 + gate2 * acc
    if final_norm:
        x2 = _rms_norm(x2, gfin_ref[...])
    o_ref[...] = x2


def _out_and_mlp(x, yf, yl, pf, pb, cf, cb, mod, layer, row0, g_mlp, w_out, w_ff1, w_ff2, g_final,
                 final_norm):
    b, s, _ = x.shape
    segs = ROW_TILE // SEG
    x_spec = pl.BlockSpec((None, ROW_TILE, D_MODEL), lambda bi, i: (bi, i, 0))
    slab_spec = _slab_spec()
    c_spec = pl.BlockSpec((None, segs, D_LRU), lambda bi, i: (bi, i, 0))
    return pl.pallas_call(
        functools.partial(_ffn_kernel, final_norm=final_norm),
        grid=(b, s // ROW_TILE),
        in_specs=[
            x_spec,
            pl.BlockSpec((None, ROW_TILE, D_FOURIER), lambda bi, i: (bi, i, 0)),
            slab_spec, slab_spec, slab_spec, c_spec, c_spec,
            _mod_spec(layer, row0),
            _const_spec((1, D_MODEL)),
            _const_spec((D_MODEL, D_MODEL)),
            _const_spec((D_MODEL, D_FF)),
            _const_spec((D_FF, D_MODEL)),
            _const_spec((1, D_MODEL)),
        ],
        out_specs=x_spec,
        out_shape=jax.ShapeDtypeStruct(x.shape, F32),
        scratch_shapes=[pltpu.VMEM((ROW_TILE, D_LRU), BF16)],
        compiler_params=_params("parallel", "parallel"),
        name="outproj_mlp",
    )(x, yf, yl, pf, pb, cf, cb, mod, g_mlp.reshape(1, D_MODEL), w_out, w_ff1, w_ff2,
      g_final.reshape(1, D_MODEL))


def _gate_weights(w_rgate, b_rgate, w_igate, b_igate):
    depth = w_rgate.shape[0]
    n_half = D_LRU // MXU_K
    heads_per_half = N_HEADS // n_half
    w_halves, b_halves = [], []
    for hh in range(n_half):
        hs = slice(hh * heads_per_half, (hh + 1) * heads_per_half)
        w_cols, b_cols = [], []
        for d in range(2):
            for w, bias in ((w_rgate, b_rgate), (w_igate, b_igate)):
                w_cols.append(_block_diag(w[:, d, hs]))
                b_cols.append(bias[:, d, hs].reshape(depth, MXU_K))
        w_halves.append(jnp.concatenate(w_cols, axis=-1))
        b_halves.append(jnp.concatenate(b_cols, axis=-1))
    wg = (0.5 * jnp.stack(w_halves, axis=1)).astype(BF16)
    bg = (0.5 * jnp.concatenate(b_halves, axis=-1)).reshape(depth, 1, n_half * GATE_COLS)
    return wg, bg


def _trunk(x, mod, row0, layers, g_final):
    depth = len(layers)
    for l, p in enumerate(layers):
        ua, uv, ur, gg = _in_projection(x, mod, l, row0, p["g_mix"], p["w_fold"])
        yf = _fourier_mix(ua, uv)
        yl, pf, pb, ah = _lru_local(ur, gg, p["conv_w"], p["conv_b"], p["wg"], p["bg"], p["lam"])
        cf, cb = _segment_carries(ah)
        x = _out_and_mlp(x, yf, yl, pf, pb, cf, cb, mod, l, row0, p["g_mlp"], p["w_out"],
                         p["w_ff1"], p["w_ff2"], g_final, final_norm=(l == depth - 1))
    return x


def kernel(x_prompt, x_sample, c_prompt, c_sample, g_mix, g_mlp, w_mod, b_mod, w_in, w_fourier,
           conv_w, conv_b, w_rgate, b_rgate, w_igate, b_igate, lru_lambda, w_out, w_ff1, w_ff2,
           g_final):
    depth = w_in.shape[0]
    n_prompt = c_prompt.shape[0]
    n_cond = n_prompt + c_sample.shape[0]
    pad = -n_cond % SUBLANES
    c_all = jnp.concatenate([c_prompt, c_sample, jnp.zeros((pad, D_MODEL), F32)], axis=0)
    mod = _modulation(c_all, w_mod, b_mod)
    mod = mod.reshape(depth, n_cond + pad, 1, N_MOD * D_MODEL)

    w_fold = _fold_in_weights(w_in, w_fourier)
    wg, bg = _gate_weights(w_rgate, b_rgate, w_igate, b_igate)
    w_out_b = w_out.astype(BF16)
    w_ff1_b = w_ff1.astype(BF16)
    w_ff2_b = w_ff2.astype(BF16)
    layers = [dict(g_mix=g_mix[l], g_mlp=g_mlp[l], w_fold=w_fold[l], conv_w=conv_w[l], conv_b=conv_b[l],
                   wg=wg[l], bg=bg[l], lam=lru_lambda[l], w_out=w_out_b[l], w_ff1=w_ff1_b[l],
                   w_ff2=w_ff2_b[l]) for l in range(depth)]
    y_prompt = _trunk(x_prompt, mod, 0, layers, g_final)
    y_sample = _trunk(x_sample, mod, n_prompt, layers, g_final)
    return (y_prompt, y_sample)
```

```python
import functools
import math

import numpy as np
import jax
import jax.numpy as jnp
from jax import lax
from jax.experimental import pallas as pl
from jax.experimental.pallas import tpu as pltpu

F32 = jnp.float32
BF16 = jnp.bfloat16
HIGHEST = lax.Precision.HIGHEST

D_MODEL = 1024
HEAD_DIM = 64
D_FOURIER = 512
D_LRU = 512
N_HEADS = 8
D_FF = 4096
N_MOD = 6
EPS = 1e-6
LRU_C = 8.0
CONV_WIDTH = 4

LANES = 128
SUBLANES = 8
BF16_ROWS = 16
N_SLAB = D_LRU // LANES
ROW_TILE = 512
SEG = ROW_TILE // SUBLANES
SEG_PITCH = SEG + SUBLANES
TILE_PITCH = SUBLANES * SEG_PITCH
LOG2E = 1.4426950408889634
MXU_K = 256
GATE_COLS = 4 * MXU_K
SCAN_UNROLL = 2
DFT_N2 = 256
DFT_GROUP = BF16_ROWS
FF_CHUNK = 1024
MOD_COLS = 1536
VMEM_LIMIT = 56 * 1024 * 1024


def _const_spec(shape):
    n = len(shape)
    return pl.BlockSpec(shape, lambda *_: (0,) * n, pipeline_mode=pl.Buffered(1))


def _params(*sem):
    return pltpu.CompilerParams(dimension_semantics=sem, vmem_limit_bytes=VMEM_LIMIT)


def _mod_kernel(c_ref, w_ref, b_ref, o_ref):
    c = c_ref[...]
    s = c * jax.nn.sigmoid(c)
    o_ref[...] = jnp.dot(s, w_ref[...], precision=HIGHEST, preferred_element_type=F32) + b_ref[...]


def _modulation(c_all, w_mod, b_mod):
    depth = w_mod.shape[0]
    n_out = N_MOD * D_MODEL
    rows = c_all.shape[0]
    return pl.pallas_call(
        _mod_kernel,
        grid=(depth, n_out // MOD_COLS),
        in_specs=[
            pl.BlockSpec((rows, D_MODEL), lambda l, j: (0, 0)),
            pl.BlockSpec((None, D_MODEL, MOD_COLS), lambda l, j: (l, 0, j)),
            pl.BlockSpec((None, 1, MOD_COLS), lambda l, j: (l, 0, j)),
        ],
        out_specs=pl.BlockSpec((None, rows, MOD_COLS), lambda l, j: (l, 0, j)),
        out_shape=jax.ShapeDtypeStruct((depth, rows, n_out), F32),
        compiler_params=_params("parallel", "parallel"),
        name="adaln_mod",
    )(c_all, w_mod, b_mod.reshape(depth, 1, n_out))


def _prep_kernel(win_ref, bdwf_ref, bdc_ref, bds_ref, o_ref):
    wf = bdwf_ref[...]
    ma = jnp.dot(bdc_ref[...], wf, precision=HIGHEST, preferred_element_type=F32)
    mv = -jnp.dot(bds_ref[...], wf, precision=HIGHEST, preferred_element_type=F32)
    wi = win_ref[:, 0:D_FOURIER]
    o_ref[:, 0:D_FOURIER] = jnp.dot(wi, ma, precision=HIGHEST, preferred_element_type=F32).astype(BF16)
    o_ref[:, D_FOURIER:2 * D_FOURIER] = jnp.dot(
        wi, mv, precision=HIGHEST, preferred_element_type=F32).astype(BF16)
    o_ref[:, 2 * D_FOURIER:] = win_ref[:, D_FOURIER:].astype(BF16)


def _block_diag(w):
    h, d, e = w.shape[-3:]
    eye = jnp.eye(h, dtype=w.dtype)
    out = w[..., :, :, None, :] * eye[:, None, :, None]
    return out.reshape(w.shape[:-3] + (h * d, h * e))


def _channel_dft_tables():
    k = np.arange(HEAD_DIM)
    ang = 2.0 * np.pi * np.outer(k, k) / HEAD_DIM
    eye = np.eye(N_HEADS)
    return (jnp.asarray(np.kron(eye, np.cos(ang)), F32), jnp.asarray(np.kron(eye, np.sin(ang)), F32))


def _fold_in_weights(w_in, w_fourier):
    depth = w_in.shape[0]
    d_in = w_in.shape[2]
    n_out = d_in + D_FOURIER
    bdc, bds = _channel_dft_tables()
    bdwf = _block_diag(w_fourier)
    return pl.pallas_call(
        _prep_kernel,
        grid=(depth,),
        in_specs=[
            pl.BlockSpec((None, D_MODEL, d_in), lambda l: (l, 0, 0)),
            pl.BlockSpec((None, D_FOURIER, D_FOURIER), lambda l: (l, 0, 0)),
            _const_spec((D_FOURIER, D_FOURIER)),
            _const_spec((D_FOURIER, D_FOURIER)),
        ],
        out_specs=pl.BlockSpec((None, D_MODEL, n_out), lambda l: (l, 0, 0)),
        out_shape=jax.ShapeDtypeStruct((depth, D_MODEL, n_out), BF16),
        compiler_params=_params("parallel"),
        name="fold_in_weights",
    )(w_in, bdwf, bdc, bds)


def _rms_norm(x, g):
    ms = jnp.mean(x * x, axis=-1, keepdims=True)
    return x * lax.rsqrt(ms + EPS) * g


def _gelu_tanh(x):
    c = math.sqrt(2.0 / math.pi)
    return 0.5 * x * (1.0 + jnp.tanh(c * (x + 0.044715 * (x * x * x))))


def _mod_spec(layer, row0):
    return pl.BlockSpec((None, None, 1, N_MOD * D_MODEL), lambda b, i: (layer, row0 + b, 0, 0))


def _inproj_kernel(x_ref, mod_ref, g_ref, w_ref, ua_ref, uv_ref, ur_ref, gg_ref, *, n1):
    shift = mod_ref[:, 0:D_MODEL]
    scale = mod_ref[:, D_MODEL:2 * D_MODEL]
    h = _rms_norm(x_ref[...], g_ref[...]) * (1.0 + scale) + shift
    z = jnp.dot(h.astype(BF16), w_ref[...], preferred_element_type=F32)
    n_grp = n1 // SUBLANES
    for c in range(N_SLAB):
        for s2 in range(ROW_TILE // n1):
            for grp in range(n_grp):
                src = slice(s2 * n1 + grp * SUBLANES, s2 * n1 + (grp + 1) * SUBLANES)
                dst = slice(s2 * SUBLANES, (s2 + 1) * SUBLANES)
                ua_ref[c, grp, dst, :] = z[src, c * LANES:(c + 1) * LANES]
                uv_ref[c, grp, dst, :] = z[src, D_FOURIER + c * LANES:D_FOURIER + (c + 1) * LANES]
    r0 = 2 * D_FOURIER
    g0 = r0 + D_LRU
    pad = jnp.zeros((SEG_PITCH - SEG, LANES), F32)
    for c in range(N_SLAB):
        ur = z[:, r0 + c * LANES:r0 + (c + 1) * LANES]
        gg = _gelu_tanh(z[:, g0 + c * LANES:g0 + (c + 1) * LANES])
        for j in range(SUBLANES):
            ur_ref[c, j * SEG_PITCH:j * SEG_PITCH + SEG, :] = ur[j * SEG:(j + 1) * SEG]
            gg_ref[c, j * SEG_PITCH:j * SEG_PITCH + SEG, :] = gg[j * SEG:(j + 1) * SEG]
            ur_ref[c, j * SEG_PITCH + SEG:(j + 1) * SEG_PITCH, :] = pad
            gg_ref[c, j * SEG_PITCH + SEG:(j + 1) * SEG_PITCH, :] = pad


def _slab_shape(b, s):
    return jax.ShapeDtypeStruct((b, N_SLAB, s // ROW_TILE * TILE_PITCH, LANES), F32)


def _slab_spec():
    return pl.BlockSpec((None, N_SLAB, TILE_PITCH, LANES), lambda bi, i: (bi, 0, i, 0))


def _in_projection(x, mod, layer, row0, g_mix, w_fold):
    b, s, _ = x.shape
    n_out = w_fold.shape[-1]
    slab = _slab_shape(b, s)
    slab_spec = _slab_spec()
    n1 = s // DFT_N2
    n_grp = n1 // SUBLANES
    grp_rows = ROW_TILE // n1 * SUBLANES
    half = jax.ShapeDtypeStruct((b, N_SLAB, n_grp, DFT_N2 * SUBLANES, LANES), F32)
    half_spec = pl.BlockSpec((None, N_SLAB, n_grp, grp_rows, LANES), lambda bi, i: (bi, 0, 0, i, 0))
    return pl.pallas_call(
        functools.partial(_inproj_kernel, n1=n1),
        grid=(b, s // ROW_TILE),
        in_specs=[
            pl.BlockSpec((None, ROW_TILE, D_MODEL), lambda bi, i: (bi, i, 0)),
            _mod_spec(layer, row0),
            _const_spec((1, D_MODEL)),
            _const_spec((D_MODEL, n_out)),
        ],
        out_specs=[half_spec, half_spec, slab_spec, slab_spec],
        out_shape=[half, half, slab, slab],
        compiler_params=_params("parallel", "parallel"),
        name="inproj",
    )(x, mod, g_mix.reshape(1, D_MODEL), w_fold)


def _dft_tables(seq):
    n2 = DFT_N2
    n1 = seq // n2
    k2 = np.arange(n2)
    ang = 2.0 * np.pi * np.outer(k2, k2) / n2
    c2 = np.cos(ang) / math.sqrt(n2)
    s2 = np.sin(ang) / math.sqrt(n2)
    fa = np.concatenate([c2, -s2], axis=0)
    fv = np.concatenate([s2, c2], axis=0)
    g = DFT_GROUP
    n_groups = n2 // g
    rows = n1 * g
    k1 = np.arange(n1)
    s1 = np.arange(n1)
    kp = np.arange(g)
    scale = 1.0 / math.sqrt(n1 * HEAD_DIM)
    lc = np.zeros((n_groups, n1, g, n1, g), np.float64)
    ls = np.zeros((n_groups, n1, g, n1, g), np.float64)
    for grp in range(n_groups):
        k = n2 * k1[:, None] + (g * grp + kp)[None, :]
        ang_b = 2.0 * np.pi * (k[:, :, None] * s1[None, None, :] % seq) / seq
        for q in range(g):
            lc[grp, :, q, :, q] = np.cos(ang_b[:, q, :]) * scale
            ls[grp, :, q, :, q] = np.sin(ang_b[:, q, :]) * scale
    lc = lc.reshape(n_groups, rows, rows)
    ls = ls.reshape(n_groups, rows, rows)
    as_bf16 = lambda a: jnp.asarray(a.astype(np.float32)).astype(BF16)
    return as_bf16(fa), as_bf16(fv), as_bf16(lc), as_bf16(ls)


def _dft_kernel(a_ref, v_ref, fa_ref, fv_ref, lc_ref, ls_ref, y_ref, tre_scr, tim_scr, *, n1):
    k = pl.program_id(1)
    n_grp = n1 // SUBLANES

    @pl.when(k < n_grp)
    def _():
        for q in range(SUBLANES):
            rows = pl.ds(q, DFT_N2, stride=SUBLANES)
            a = jnp.concatenate([a_ref[c, rows, :] for c in range(N_SLAB)], axis=1).astype(BF16)
            v = jnp.concatenate([v_ref[c, rows, :] for c in range(N_SLAB)], axis=1).astype(BF16)
            o = (jnp.dot(fa_ref[...], a, preferred_element_type=F32)
                 + jnp.dot(fv_ref[...], v, preferred_element_type=F32))
            tre_scr[k * SUBLANES + q] = o[0:DFT_N2].astype(BF16)
            tim_scr[k * SUBLANES + q] = o[DFT_N2:].astype(BF16)

    @pl.when(k >= n_grp)
    def _():
        rows = n1 * DFT_GROUP
        grp = pl.ds(pl.multiple_of((k - n_grp) * DFT_GROUP, DFT_GROUP), DFT_GROUP)
        tr = tre_scr[:, grp, :].reshape(rows, D_FOURIER)
        ti = tim_scr[:, grp, :].reshape(rows, D_FOURIER)
        o = (jnp.dot(lc_ref[...], tr, preferred_element_type=F32)
             + jnp.dot(ls_ref[...], ti, preferred_element_type=F32))
        y_ref[...] = o.astype(BF16).reshape(n1, DFT_GROUP, D_FOURIER)


def _fourier_mix(ua, uv, s):
    b = ua.shape[0]
    n2 = DFT_N2
    n1 = s // n2
    n_grp = n1 // SUBLANES
    n_out = n2 // DFT_GROUP
    rows = n1 * DFT_GROUP
    fa, fv, lc, ls = _dft_tables(s)
    in_spec = pl.BlockSpec((None, N_SLAB, None, n2 * SUBLANES, LANES),
                           lambda bi, k: (bi, 0, jnp.minimum(k, n_grp - 1), 0, 0))
    l_spec = pl.BlockSpec((None, rows, rows), lambda bi, k: (jnp.maximum(k - n_grp, 0), 0, 0))
    t_scr = pltpu.VMEM((n1, n2, D_FOURIER), BF16)
    y = pl.pallas_call(
        functools.partial(_dft_kernel, n1=n1),
        grid=(b, n_grp + n_out),
        in_specs=[in_spec, in_spec, _const_spec((2 * n2, n2)), _const_spec((2 * n2, n2)), l_spec, l_spec],
        out_specs=pl.BlockSpec((None, n1, DFT_GROUP, D_FOURIER),
                               lambda bi, k: (bi, 0, jnp.maximum(k - n_grp, 0), 0)),
        out_shape=jax.ShapeDtypeStruct((b, n1, n2, D_FOURIER), BF16),
        scratch_shapes=[t_scr, t_scr],
        compiler_params=_params("parallel", "arbitrary"),
        name="seq_dft",
    )(ua, uv, fa, fv, lc, ls)
    return y.reshape(b, s, D_FOURIER)


def _segment_rows(ref, c, t):
    return ref[c, pl.ds(t, SUBLANES, stride=SEG_PITCH), :]


def _lru_kernel(ur_ref, prev_ref, next_ref, gg_ref, cw_ref, cb_ref, wg_ref, bg_ref, lam_ref,
                yl_ref, pf_ref, pb_ref, ah_ref, uc_scr, pre_scr, hf_scr):
    i = pl.program_id(1)
    n_tiles = pl.num_programs(1)
    sub = lax.broadcasted_iota(jnp.int32, (SUBLANES, LANES), 0)
    slab_lanes = [slice(c * LANES, (c + 1) * LANES) for c in range(N_SLAB)]

    for c in range(N_SLAB):
        lanes = slab_lanes[c]
        prev_rows = [jnp.where(i > 0, prev_ref[c, SUBLANES - 2 + q:SUBLANES - 1 + q, :], 0.0) for q in range(2)]
        next_row = jnp.where(i < n_tiles - 1, next_ref[c, 0:1, :], 0.0)
        first = _segment_rows(ur_ref, c, 0)
        window = [jnp.where(sub == 0, prev_rows[q], pltpu.roll(_segment_rows(ur_ref, c, SEG - 2 + q), 1, 0))
                  for q in range(2)] + [first]
        cw = [0.5 * cw_ref[k:k + 1, lanes] for k in range(CONV_WIDTH)]
        cbias = 0.5 * cb_ref[:, lanes]
        for t in range(SEG):
            if t + 1 < SEG:
                nxt = _segment_rows(ur_ref, c, t + 1)
            else:
                nxt = jnp.where(sub == SUBLANES - 1, next_row, pltpu.roll(first, SUBLANES - 1, 0))
            window.append(nxt)
            uc = cbias + window[0] * cw[0] + window[1] * cw[1] + window[2] * cw[2] + window[3] * cw[3]
            uc_scr[t * SUBLANES:(t + 1) * SUBLANES, lanes] = uc
            window.pop(0)

    for hh in range(D_LRU // MXU_K):
        cols = slice(hh * GATE_COLS, (hh + 1) * GATE_COLS)
        lhs = uc_scr[:, hh * MXU_K:(hh + 1) * MXU_K].astype(BF16)
        pre_scr[:, cols] = jnp.dot(lhs, wg_ref[hh], preferred_element_type=F32) + bg_ref[:, cols]

    neg_lam = -lam_ref[...]
    softplus = jnp.maximum(neg_lam, 0.0) + jnp.log1p(jnp.exp(-jnp.abs(neg_lam)))
    k_all = (-0.5 * LRU_C * LOG2E) * softplus
    k_rows = [[jnp.broadcast_to(k_all[d:d + 1, slab_lanes[c]], (SUBLANES, LANES)) for c in range(N_SLAB)]
              for d in range(2)]

    def scan_inputs(rows, c, d):
        col = (c * LANES // MXU_K) * GATE_COLS + d * 2 * MXU_K + (c * LANES) % MXU_K
        th_r = jnp.tanh(pre_scr[rows, col:col + LANES])
        th_i = jnp.tanh(pre_scr[rows, col + MXU_K:col + MXU_K + LANES])
        k = k_rows[d][c]
        a = jnp.exp2(k * th_r + k)
        m = 1.0 - a * a
        mult = jnp.where(m > 0.0, m * lax.rsqrt(m), 0.0)
        return a, mult * ((th_i + 1.0) * uc_scr[rows, slab_lanes[c]])

    zeros = tuple(jnp.zeros((SUBLANES, LANES), F32) for _ in range(N_SLAB))
    ones = tuple(jnp.ones((SUBLANES, LANES), F32) for _ in range(N_SLAB))

    def fwd_step(t, carry):
        hs, ps = carry
        rows = pl.ds(pl.multiple_of(t * SUBLANES, SUBLANES), SUBLANES)
        new_h, new_p = [], []
        for c in range(N_SLAB):
            a, b = scan_inputs(rows, c, 0)
            h = a * hs[c] + b
            p = a * ps[c]
            hf_scr[rows, slab_lanes[c]] = h
            pf_ref[c, pl.ds(t, SUBLANES, stride=SEG_PITCH), :] = p * _segment_rows(gg_ref, c, t)
            new_h.append(h)
            new_p.append(p)
        return tuple(new_h), tuple(new_p)

    h_f, p_f = lax.fori_loop(0, SEG, fwd_step, (zeros, ones), unroll=SCAN_UNROLL)

    def bwd_step(n, carry):
        hs, ps = carry
        t = SEG - 1 - n
        rows = pl.ds(pl.multiple_of(t * SUBLANES, SUBLANES), SUBLANES)
        new_h, new_p = [], []
        for c in range(N_SLAB):
            a, b = scan_inputs(rows, c, 1)
            h = a * hs[c] + b
            p = a * ps[c]
            g = _segment_rows(gg_ref, c, t)
            yl_ref[c, pl.ds(t, SUBLANES, stride=SEG_PITCH), :] = (hf_scr[rows, slab_lanes[c]] + h) * g
            pb_ref[c, pl.ds(t, SUBLANES, stride=SEG_PITCH), :] = p * g
            new_h.append(h)
            new_p.append(p)
        return tuple(new_h), tuple(new_p)

    h_b, p_b = lax.fori_loop(0, SEG, bwd_step, (zeros, ones), unroll=SCAN_UNROLL)

    pad = jnp.zeros((SEG_PITCH - SEG, LANES), F32)
    for c in range(N_SLAB):
        ah_ref[0, :, slab_lanes[c]] = p_f[c]
        ah_ref[1, :, slab_lanes[c]] = h_f[c]
        ah_ref[2, :, slab_lanes[c]] = p_b[c]
        ah_ref[3, :, slab_lanes[c]] = h_b[c]
        for j in range(SUBLANES):
            pad_rows = slice(j * SEG_PITCH + SEG, (j + 1) * SEG_PITCH)
            yl_ref[c, pad_rows, :] = pad
            pf_ref[c, pad_rows, :] = pad
            pb_ref[c, pad_rows, :] = pad


def _lru_local(ur, gg, conv_w, conv_b, wg, bg, lam):
    b, _, rows, _ = ur.shape
    n_tiles = rows // TILE_PITCH
    blocks_per_tile = TILE_PITCH // SUBLANES
    last_block = rows // SUBLANES - 1
    s = n_tiles * ROW_TILE
    slab = _slab_shape(b, s)
    slab_spec = _slab_spec()
    prev_spec = pl.BlockSpec((None, N_SLAB, SUBLANES, LANES),
                             lambda bi, i: (bi, 0, jnp.maximum(i * blocks_per_tile - 2, 0), 0))
    next_spec = pl.BlockSpec((None, N_SLAB, SUBLANES, LANES),
                             lambda bi, i: (bi, 0, jnp.minimum((i + 1) * blocks_per_tile, last_block), 0))
    n_seg = s // SEG
    n_half = D_LRU // MXU_K
    tile_f32 = pltpu.VMEM((ROW_TILE, D_LRU), F32)
    return pl.pallas_call(
        _lru_kernel,
        grid=(b, n_tiles),
        in_specs=[
            slab_spec, prev_spec, next_spec, slab_spec,
            _const_spec((CONV_WIDTH, D_LRU)),
            _const_spec((1, D_LRU)),
            _const_spec((n_half, MXU_K, GATE_COLS)),
            _const_spec((1, n_half * GATE_COLS)),
            _const_spec((2, D_LRU)),
        ],
        out_specs=[slab_spec, slab_spec, slab_spec,
                   pl.BlockSpec((None, 4, SUBLANES, D_LRU), lambda bi, i: (bi, 0, i, 0))],
        out_shape=[slab, slab, slab, jax.ShapeDtypeStruct((b, 4, n_seg, D_LRU), F32)],
        scratch_shapes=[tile_f32, pltpu.VMEM((ROW_TILE, n_half * GATE_COLS), F32), tile_f32],
        compiler_params=_params("parallel", "parallel"),
        name="lru_local",
    )(ur, ur, ur, gg, conv_w, conv_b.reshape(1, D_LRU), wg, bg, lam)


def _carry_kernel(ah_ref, cf_ref, cb_ref, *, n_seg):
    def fwd(sg, c):
        cf_ref[pl.ds(sg, 1), :] = c
        return ah_ref[1, pl.ds(sg, 1), :] + ah_ref[0, pl.ds(sg, 1), :] * c

    def bwd(n, c):
        sg = n_seg - 1 - n
        cb_ref[pl.ds(sg, 1), :] = c
        return ah_ref[3, pl.ds(sg, 1), :] + ah_ref[2, pl.ds(sg, 1), :] * c

    zero = jnp.zeros((1, D_LRU), F32)
    lax.fori_loop(0, n_seg, fwd, zero)
    lax.fori_loop(0, n_seg, bwd, zero)


def _segment_carries(ah):
    b, _, n_seg, _ = ah.shape
    c_shape = jax.ShapeDtypeStruct((b, n_seg, D_LRU), F32)
    c_spec = pl.BlockSpec((None, n_seg, D_LRU), lambda bi: (bi, 0, 0))
    return pl.pallas_call(
        functools.partial(_carry_kernel, n_seg=n_seg),
        grid=(b,),
        in_specs=[pl.BlockSpec((None, 4, n_seg, D_LRU), lambda bi: (bi, 0, 0, 0))],
        out_specs=[c_spec, c_spec],
        out_shape=[c_shape, c_shape],
        compiler_params=_params("parallel"),
        name="lru_carry",
    )(ah)


def _ffn_kernel(x_ref, yf_ref, yl_ref, pf_ref, pb_ref, cf_ref, cb_ref, mod_ref, g_ref,
                wo_ref, w1_ref, w2_ref, gfin_ref, o_ref, yr_scr, *, final_norm):
    for c in range(N_SLAB):
        lanes = slice(c * LANES, (c + 1) * LANES)
        for j in range(ROW_TILE // SEG):
            rows = slice(j * SEG_PITCH, j * SEG_PITCH + SEG)
            yr = (yl_ref[c, rows, :] + pf_ref[c, rows, :] * cf_ref[j:j + 1, lanes]
                  + pb_ref[c, rows, :] * cb_ref[j:j + 1, lanes])
            yr_scr[j * SEG:(j + 1) * SEG, lanes] = yr.astype(BF16)
    gate1 = mod_ref[:, 2 * D_MODEL:3 * D_MODEL]
    shift2 = mod_ref[:, 3 * D_MODEL:4 * D_MODEL]
    scale2 = mod_ref[:, 4 * D_MODEL:5 * D_MODEL]
    gate2 = mod_ref[:, 5 * D_MODEL:6 * D_MODEL]
    mix = (jnp.dot(yf_ref[...], wo_ref[0:D_FOURIER, :], preferred_element_type=F32)
           + jnp.dot(yr_scr[...], wo_ref[D_FOURIER:, :], preferred_element_type=F32))
    x1 = x_ref[...] + gate1 * mix
    h = (_rms_norm(x1, g_ref[...]) * (1.0 + scale2) + shift2).astype(BF16)
    acc = jnp.zeros((ROW_TILE, D_MODEL), F32)
    for k in range(D_FF // FF_CHUNK):
        cols = slice(k * FF_CHUNK, (k + 1) * FF_CHUNK)
        t = jnp.maximum(jnp.dot(h, w1_ref[:, cols], preferred_element_type=F32), 0.0)
        acc = acc + jnp.dot((t * t).astype(BF16), w2_ref[cols, :], preferred_element_type=F32)
    x2 = x1 + gate2 * acc
    if final_norm:
        x2 = _rms_norm(x2, gfin_ref[...])
    o_ref[...] = x2


def _out_and_mlp(x, yf, yl, pf, pb, cf, cb, mod, layer, row0, g_mlp, w_out, w_ff1, w_ff2, g_final,
                 final_norm):
    b, s, _ = x.shape
    segs = ROW_TILE // SEG
    x_spec = pl.BlockSpec((None, ROW_TILE, D_MODEL), lambda bi, i: (bi, i, 0))
    slab_spec = _slab_spec()
    c_spec = pl.BlockSpec((None, segs, D_LRU), lambda bi, i: (bi, i, 0))
    return pl.pallas_call(
        functools.partial(_ffn_kernel, final_norm=final_norm),
        grid=(b, s // ROW_TILE),
        in_specs=[
            x_spec,
            pl.BlockSpec((None, ROW_TILE, D_FOURIER), lambda bi, i: (bi, i, 0)),
            slab_spec, slab_spec, slab_spec, c_spec, c_spec,
            _mod_spec(layer, row0),
            _const_spec((1, D_MODEL)),
            _const_spec((D_MODEL, D_MODEL)),
            _const_spec((D_MODEL, D_FF)),
            _const_spec((D_FF, D_MODEL)),
            _const_spec((1, D_MODEL)),
        ],
        out_specs=x_spec,
        out_shape=jax.ShapeDtypeStruct(x.shape, F32),
        scratch_shapes=[pltpu.VMEM((ROW_TILE, D_LRU), BF16)],
        compiler_params=_params("parallel", "parallel"),
        name="outproj_mlp",
    )(x, yf, yl, pf, pb, cf, cb, mod, g_mlp.reshape(1, D_MODEL), w_out, w_ff1, w_ff2,
      g_final.reshape(1, D_MODEL))


def _gate_weights(w_rgate, b_rgate, w_igate, b_igate):
    depth = w_rgate.shape[0]
    n_half = D_LRU // MXU_K
    heads_per_half = N_HEADS // n_half
    w_halves, b_halves = [], []
    for hh in range(n_half):
        hs = slice(hh * heads_per_half, (hh + 1) * heads_per_half)
        w_cols, b_cols = [], []
        for d in range(2):
            for w, bias in ((w_rgate, b_rgate), (w_igate, b_igate)):
                w_cols.append(_block_diag(w[:, d, hs]))
                b_cols.append(bias[:, d, hs].reshape(depth, MXU_K))
        w_halves.append(jnp.concatenate(w_cols, axis=-1))
        b_halves.append(jnp.concatenate(b_cols, axis=-1))
    wg = jnp.stack(w_halves, axis=1).astype(BF16)
    bg = (0.5 * jnp.concatenate(b_halves, axis=-1)).reshape(depth, 1, n_half * GATE_COLS)
    return wg, bg


def _trunk(x, mod, row0, layers, g_final):
    depth = len(layers)
    for l, p in enumerate(layers):
        ua, uv, ur, gg = _in_projection(x, mod, l, row0, p["g_mix"], p["w_fold"])
        yf = _fourier_mix(ua, uv, x.shape[1])
        yl, pf, pb, ah = _lru_local(ur, gg, p["conv_w"], p["conv_b"], p["wg"], p["bg"], p["lam"])
        cf, cb = _segment_carries(ah)
        x = _out_and_mlp(x, yf, yl, pf, pb, cf, cb, mod, l, row0, p["g_mlp"], p["w_out"],
                         p["w_ff1"], p["w_ff2"], g_final, final_norm=(l == depth - 1))
    return x


def kernel(x_prompt, x_sample, c_prompt, c_sample, g_mix, g_mlp, w_mod, b_mod, w_in, w_fourier,
           conv_w, conv_b, w_rgate, b_rgate, w_igate, b_igate, lru_lambda, w_out, w_ff1, w_ff2,
           g_final):
    depth = w_in.shape[0]
    n_prompt = c_prompt.shape[0]
    n_cond = n_prompt + c_sample.shape[0]
    pad = -n_cond % SUBLANES
    c_all = jnp.concatenate([c_prompt, c_sample, jnp.zeros((pad, D_MODEL), F32)], axis=0)
    mod = _modulation(c_all, w_mod, b_mod)
    mod = mod.reshape(depth, n_cond + pad, 1, N_MOD * D_MODEL)

    w_fold = _fold_in_weights(w_in, w_fourier)
    wg, bg = _gate_weights(w_rgate, b_rgate, w_igate, b_igate)
    w_out_b = w_out.astype(BF16)
    w_ff1_b = w_ff1.astype(BF16)
    w_ff2_b = w_ff2.astype(BF16)
    layers = [dict(g_mix=g_mix[l], g_mlp=g_mlp[l], w_fold=w_fold[l], conv_w=conv_w[l], conv_b=conv_b[l],
                   wg=wg[l], bg=bg[l], lam=lru_lambda[l], w_out=w_out_b[l], w_ff1=w_ff1_b[l],
                   w_ff2=w_ff2_b[l]) for l in range(depth)]
    y_prompt = _trunk(x_prompt, mod, 0, layers, g_final)
    y_sample = _trunk(x_sample, mod, n_prompt, layers, g_final)
    return (y_prompt, y_sample)
```

```python
import functools
import math

import numpy as np
import jax
import jax.numpy as jnp
from jax import lax
from jax.experimental import pallas as pl
from jax.experimental.pallas import tpu as pltpu

F32 = jnp.float32
BF16 = jnp.bfloat16
HIGHEST = lax.Precision.HIGHEST

D_MODEL = 1024
HEAD_DIM = 64
D_FOURIER = 512
D_LRU = 512
N_HEADS = 8
D_FF = 4096
N_MOD = 6
EPS = 1e-6
LRU_C = 8.0
CONV_WIDTH = 4

LANES = 128
SUBLANES = 8
BF16_ROWS = 16
N_SLAB = D_LRU // LANES
ROW_TILE = 512
SEG = ROW_TILE // SUBLANES
SEG_PITCH = SEG + SUBLANES
TILE_PITCH = SUBLANES * SEG_PITCH
LOG2E = 1.4426950408889634
MXU_K = 256
GATE_COLS = 4 * MXU_K
DFT_N2 = 256
DFT_GROUP = BF16_ROWS
DFT_ROWS_PER_STEP = 1024
FF_CHUNK = 1024
MOD_COLS = 1536
VMEM_LIMIT = 56 * 1024 * 1024


def _const_spec(shape):
    n = len(shape)
    return pl.BlockSpec(shape, lambda *_: (0,) * n, pipeline_mode=pl.Buffered(1))


def _params(*sem):
    return pltpu.CompilerParams(dimension_semantics=sem, vmem_limit_bytes=VMEM_LIMIT)


def _mod_kernel(c_ref, w_ref, b_ref, o_ref):
    c = c_ref[...]
    s = c * jax.nn.sigmoid(c)
    o_ref[...] = jnp.dot(s, w_ref[...], precision=HIGHEST, preferred_element_type=F32) + b_ref[...]


def _modulation(c_all, w_mod, b_mod):
    depth = w_mod.shape[0]
    n_out = N_MOD * D_MODEL
    rows = c_all.shape[0]
    return pl.pallas_call(
        _mod_kernel,
        grid=(depth, n_out // MOD_COLS),
        in_specs=[
            pl.BlockSpec((rows, D_MODEL), lambda l, j: (0, 0)),
            pl.BlockSpec((None, D_MODEL, MOD_COLS), lambda l, j: (l, 0, j)),
            pl.BlockSpec((None, 1, MOD_COLS), lambda l, j: (l, 0, j)),
        ],
        out_specs=pl.BlockSpec((None, rows, MOD_COLS), lambda l, j: (l, 0, j)),
        out_shape=jax.ShapeDtypeStruct((depth, rows, n_out), F32),
        compiler_params=_params("parallel", "parallel"),
        name="adaln_mod",
    )(c_all, w_mod, b_mod.reshape(depth, 1, n_out))


def _prep_kernel(win_ref, bdwf_ref, bdc_ref, bds_ref, o_ref):
    wf = bdwf_ref[...]
    ma = jnp.dot(bdc_ref[...], wf, precision=HIGHEST, preferred_element_type=F32)
    mv = -jnp.dot(bds_ref[...], wf, precision=HIGHEST, preferred_element_type=F32)
    wi = win_ref[:, 0:D_FOURIER]
    o_ref[:, 0:D_FOURIER] = jnp.dot(wi, ma, precision=HIGHEST, preferred_element_type=F32).astype(BF16)
    o_ref[:, D_FOURIER:2 * D_FOURIER] = jnp.dot(
        wi, mv, precision=HIGHEST, preferred_element_type=F32).astype(BF16)
    o_ref[:, 2 * D_FOURIER:] = win_ref[:, D_FOURIER:].astype(BF16)


def _block_diag(w):
    h, d, e = w.shape[-3:]
    eye = jnp.eye(h, dtype=w.dtype)
    out = w[..., :, :, None, :] * eye[:, None, :, None]
    return out.reshape(w.shape[:-3] + (h * d, h * e))


def _channel_dft_tables():
    k = np.arange(HEAD_DIM)
    ang = 2.0 * np.pi * np.outer(k, k) / HEAD_DIM
    eye = np.eye(N_HEADS)
    return (jnp.asarray(np.kron(eye, np.cos(ang)), F32), jnp.asarray(np.kron(eye, np.sin(ang)), F32))


def _fold_in_weights(w_in, w_fourier):
    depth = w_in.shape[0]
    d_in = w_in.shape[2]
    n_out = d_in + D_FOURIER
    bdc, bds = _channel_dft_tables()
    bdwf = _block_diag(w_fourier)
    return pl.pallas_call(
        _prep_kernel,
        grid=(depth,),
        in_specs=[
            pl.BlockSpec((None, D_MODEL, d_in), lambda l: (l, 0, 0)),
            pl.BlockSpec((None, D_FOURIER, D_FOURIER), lambda l: (l, 0, 0)),
            _const_spec((D_FOURIER, D_FOURIER)),
            _const_spec((D_FOURIER, D_FOURIER)),
        ],
        out_specs=pl.BlockSpec((None, D_MODEL, n_out), lambda l: (l, 0, 0)),
        out_shape=jax.ShapeDtypeStruct((depth, D_MODEL, n_out), BF16),
        compiler_params=_params("parallel"),
        name="fold_in_weights",
    )(w_in, bdwf, bdc, bds)


def _rms_norm(x, g):
    ms = jnp.mean(x * x, axis=-1, keepdims=True)
    return x * lax.rsqrt(ms + EPS) * g


def _gelu_tanh(x):
    c = math.sqrt(2.0 / math.pi)
    return 0.5 * x * (1.0 + jnp.tanh(c * (x + 0.044715 * (x * x * x))))


def _mod_spec(layer, row0):
    return pl.BlockSpec((None, None, 1, N_MOD * D_MODEL), lambda b, i: (layer, row0 + b, 0, 0))


def _slab_shape(b, s):
    return jax.ShapeDtypeStruct((b, N_SLAB, s // ROW_TILE * TILE_PITCH, LANES), F32)


def _slab_spec():
    return pl.BlockSpec((None, N_SLAB, TILE_PITCH, LANES), lambda bi, i: (bi, 0, i, 0))


def _inproj_kernel(x_ref, mod_ref, g_ref, w_ref, ua_ref, uv_ref, ur_ref, gg_ref, *, n1):
    shift = mod_ref[:, 0:D_MODEL]
    scale = mod_ref[:, D_MODEL:2 * D_MODEL]
    h = _rms_norm(x_ref[...], g_ref[...]) * (1.0 + scale) + shift
    z = jnp.dot(h.astype(BF16), w_ref[...], preferred_element_type=F32)
    n_grp = n1 // SUBLANES
    for c in range(N_SLAB):
        for s2 in range(ROW_TILE // n1):
            for grp in range(n_grp):
                src = slice(s2 * n1 + grp * SUBLANES, s2 * n1 + (grp + 1) * SUBLANES)
                dst = slice(s2 * SUBLANES, (s2 + 1) * SUBLANES)
                ua_ref[c, grp, dst, :] = z[src, c * LANES:(c + 1) * LANES]
                uv_ref[c, grp, dst, :] = z[src, D_FOURIER + c * LANES:D_FOURIER + (c + 1) * LANES]
    r0 = 2 * D_FOURIER
    g0 = r0 + D_LRU
    pad = jnp.zeros((SEG_PITCH - SEG, LANES), F32)
    for c in range(N_SLAB):
        ur = z[:, r0 + c * LANES:r0 + (c + 1) * LANES]
        gg = _gelu_tanh(z[:, g0 + c * LANES:g0 + (c + 1) * LANES])
        for j in range(SUBLANES):
            ur_ref[c, j * SEG_PITCH:j * SEG_PITCH + SEG, :] = ur[j * SEG:(j + 1) * SEG]
            gg_ref[c, j * SEG_PITCH:j * SEG_PITCH + SEG, :] = gg[j * SEG:(j + 1) * SEG]
            ur_ref[c, j * SEG_PITCH + SEG:(j + 1) * SEG_PITCH, :] = pad
            gg_ref[c, j * SEG_PITCH + SEG:(j + 1) * SEG_PITCH, :] = pad


def _in_projection(x, mod, layer, row0, g_mix, w_fold):
    b, s, _ = x.shape
    n_out = w_fold.shape[-1]
    slab = _slab_shape(b, s)
    slab_spec = _slab_spec()
    n1 = s // DFT_N2
    n_grp = n1 // SUBLANES
    grp_rows = ROW_TILE // n1 * SUBLANES
    half = jax.ShapeDtypeStruct((b, N_SLAB, n_grp, DFT_N2 * SUBLANES, LANES), F32)
    half_spec = pl.BlockSpec((None, N_SLAB, n_grp, grp_rows, LANES), lambda bi, i: (bi, 0, 0, i, 0))
    return pl.pallas_call(
        functools.partial(_inproj_kernel, n1=n1),
        grid=(b, s // ROW_TILE),
        in_specs=[
            pl.BlockSpec((None, ROW_TILE, D_MODEL), lambda bi, i: (bi, i, 0)),
            _mod_spec(layer, row0),
            _const_spec((1, D_MODEL)),
            _const_spec((D_MODEL, n_out)),
        ],
        out_specs=[half_spec, half_spec, slab_spec, slab_spec],
        out_shape=[half, half, slab, slab],
        compiler_params=_params("parallel", "parallel"),
        name="inproj",
    )(x, mod, g_mix.reshape(1, D_MODEL), w_fold)


def _dft_tables(seq):
    n2 = DFT_N2
    n1 = seq // n2
    k2 = np.arange(n2)
    ang = 2.0 * np.pi * np.outer(k2, k2) / n2
    c2 = np.cos(ang) / math.sqrt(n2)
    s2 = np.sin(ang) / math.sqrt(n2)
    fa = np.concatenate([c2, -s2], axis=0)
    fv = np.concatenate([s2, c2], axis=0)
    g = DFT_GROUP
    n_groups = n2 // g
    rows = n1 * g
    k1 = np.arange(n1)
    s1 = np.arange(n1)
    kp = np.arange(g)
    scale = 1.0 / math.sqrt(n1 * HEAD_DIM)
    lc = np.zeros((n_groups, n1, g, n1, g), np.float64)
    ls = np.zeros((n_groups, n1, g, n1, g), np.float64)
    for grp in range(n_groups):
        k = n2 * k1[:, None] + (g * grp + kp)[None, :]
        ang_b = 2.0 * np.pi * (k[:, :, None] * s1[None, None, :] % seq) / seq
        for q in range(g):
            lc[grp, :, q, :, q] = np.cos(ang_b[:, q, :]) * scale
            ls[grp, :, q, :, q] = np.sin(ang_b[:, q, :]) * scale
    lc = lc.reshape(n_groups, rows, rows)
    ls = ls.reshape(n_groups, rows, rows)
    as_bf16 = lambda a: jnp.asarray(a.astype(np.float32)).astype(BF16)
    return as_bf16(fa), as_bf16(fv), as_bf16(lc), as_bf16(ls)


def _dft_kernel(a_ref, v_ref, fa_ref, fv_ref, lc_ref, ls_ref, y_ref, tre_scr, tim_scr, *, n1, groups_per_step):
    k = pl.program_id(1)
    n_grp = n1 // SUBLANES

    @pl.when(k < n_grp)
    def _():
        for q in range(SUBLANES):
            rows = pl.ds(q, DFT_N2, stride=SUBLANES)
            a = jnp.concatenate([a_ref[c, rows, :] for c in range(N_SLAB)], axis=1).astype(BF16)
            v = jnp.concatenate([v_ref[c, rows, :] for c in range(N_SLAB)], axis=1).astype(BF16)
            o = (jnp.dot(fa_ref[...], a, preferred_element_type=F32)
                 + jnp.dot(fv_ref[...], v, preferred_element_type=F32))
            tre_scr[k * SUBLANES + q] = o[0:DFT_N2].astype(BF16)
            tim_scr[k * SUBLANES + q] = o[DFT_N2:].astype(BF16)

    @pl.when(k >= n_grp)
    def _():
        rows = n1 * DFT_GROUP
        for j in range(groups_per_step):
            first = pl.multiple_of(((k - n_grp) * groups_per_step + j) * DFT_GROUP, DFT_GROUP)
            tr = tre_scr[:, pl.ds(first, DFT_GROUP), :].reshape(rows, D_FOURIER)
            ti = tim_scr[:, pl.ds(first, DFT_GROUP), :].reshape(rows, D_FOURIER)
            o = (jnp.dot(lc_ref[j], tr, preferred_element_type=F32)
                 + jnp.dot(ls_ref[j], ti, preferred_element_type=F32))
            y_ref[:, j * DFT_GROUP:(j + 1) * DFT_GROUP, :] = o.astype(BF16).reshape(n1, DFT_GROUP, D_FOURIER)


def _fourier_mix(ua, uv, s):
    b = ua.shape[0]
    n2 = DFT_N2
    n1 = s // n2
    n_grp = n1 // SUBLANES
    rows = n1 * DFT_GROUP
    groups_per_step = DFT_ROWS_PER_STEP // rows
    n_out = n2 // (DFT_GROUP * groups_per_step)
    fa, fv, lc, ls = _dft_tables(s)
    in_spec = pl.BlockSpec((None, N_SLAB, None, n2 * SUBLANES, LANES),
                           lambda bi, k: (bi, 0, jnp.minimum(k, n_grp - 1), 0, 0))
    l_spec = pl.BlockSpec((groups_per_step, rows, rows), lambda bi, k: (jnp.maximum(k - n_grp, 0), 0, 0))
    t_scr = pltpu.VMEM((n1, n2, D_FOURIER), BF16)
    y = pl.pallas_call(
        functools.partial(_dft_kernel, n1=n1, groups_per_step=groups_per_step),
        grid=(b, n_grp + n_out),
        in_specs=[in_spec, in_spec, _const_spec((2 * n2, n2)), _const_spec((2 * n2, n2)), l_spec, l_spec],
        out_specs=pl.BlockSpec((None, n1, DFT_GROUP * groups_per_step, D_FOURIER),
                               lambda bi, k: (bi, 0, jnp.maximum(k - n_grp, 0), 0)),
        out_shape=jax.ShapeDtypeStruct((b, n1, n2, D_FOURIER), BF16),
        scratch_shapes=[t_scr, t_scr],
        compiler_params=_params("parallel", "arbitrary"),
        name="seq_dft",
    )(ua, uv, fa, fv, lc, ls)
    return y.reshape(b, s, D_FOURIER)


def _segment_rows(ref, c, t):
    return ref[c, pl.ds(t, SUBLANES, stride=SEG_PITCH), :]


def _lru_kernel(ur_ref, prev_ref, next_ref, gg_ref, cw_ref, cb_ref, wg_ref, bg_ref, lam_ref,
                yl_ref, pf_ref, pb_ref, ah_ref, uc_scr, pre_scr, hf_scr):
    i = pl.program_id(1)
    n_tiles = pl.num_programs(1)
    sub = lax.broadcasted_iota(jnp.int32, (SUBLANES, LANES), 0)
    slab_lanes = [slice(c * LANES, (c + 1) * LANES) for c in range(N_SLAB)]

    pad = jnp.zeros((SEG_PITCH - SEG, LANES), F32)
    for c in range(N_SLAB):
        for j in range(SUBLANES):
            pad_rows = slice(j * SEG_PITCH + SEG, (j + 1) * SEG_PITCH)
            yl_ref[c, pad_rows, :] = pad
            pf_ref[c, pad_rows, :] = pad
            pb_ref[c, pad_rows, :] = pad

    for c in range(N_SLAB):
        lanes = slab_lanes[c]
        prev_rows = [jnp.where(i > 0, prev_ref[c, SUBLANES - 2 + q:SUBLANES - 1 + q, :], 0.0) for q in range(2)]
        next_row = jnp.where(i < n_tiles - 1, next_ref[c, 0:1, :], 0.0)
        first = _segment_rows(ur_ref, c, 0)
        window = [jnp.where(sub == 0, prev_rows[q], pltpu.roll(_segment_rows(ur_ref, c, SEG - 2 + q), 1, 0))
                  for q in range(2)] + [first]
        cw = [jnp.broadcast_to(0.5 * cw_ref[k:k + 1, lanes], (SUBLANES, LANES)) for k in range(CONV_WIDTH)]
        cbias = jnp.broadcast_to(0.5 * cb_ref[:, lanes], (SUBLANES, LANES))
        for t in range(SEG):
            if t + 1 < SEG:
                nxt = _segment_rows(ur_ref, c, t + 1)
            else:
                nxt = jnp.where(sub == SUBLANES - 1, next_row, pltpu.roll(first, SUBLANES - 1, 0))
            window.append(nxt)
            uc = cbias + window[0] * cw[0] + window[1] * cw[1] + window[2] * cw[2] + window[3] * cw[3]
            uc_scr[t * SUBLANES:(t + 1) * SUBLANES, lanes] = uc
            window.pop(0)

    for hh in range(D_LRU // MXU_K):
        cols = slice(hh * GATE_COLS, (hh + 1) * GATE_COLS)
        lhs = uc_scr[:, hh * MXU_K:(hh + 1) * MXU_K].astype(BF16)
        pre_scr[:, cols] = jnp.dot(lhs, wg_ref[hh], preferred_element_type=F32) + bg_ref[:, cols]

    neg_lam = -lam_ref[...]
    softplus = jnp.maximum(neg_lam, 0.0) + jnp.log1p(jnp.exp(-jnp.abs(neg_lam)))
    k_all = (-0.5 * LRU_C * LOG2E) * softplus
    k_rows = [[jnp.broadcast_to(k_all[d:d + 1, slab_lanes[c]], (SUBLANES, LANES)) for c in range(N_SLAB)]
              for d in range(2)]

    def scan_inputs(rows, c, d):
        col = (c * LANES // MXU_K) * GATE_COLS + d * 2 * MXU_K + (c * LANES) % MXU_K
        th_r = jnp.tanh(pre_scr[rows, col:col + LANES])
        th_i = jnp.tanh(pre_scr[rows, col + MXU_K:col + MXU_K + LANES])
        k = k_rows[d][c]
        a = jnp.exp2(k * th_r + k)
        m = 1.0 - a * a
        mult = jnp.where(m > 0.0, m * lax.rsqrt(m), 0.0)
        return a, mult * ((th_i + 1.0) * uc_scr[rows, slab_lanes[c]])

    hs = [jnp.zeros((SUBLANES, LANES), F32) for _ in range(N_SLAB)]
    ps = [jnp.ones((SUBLANES, LANES), F32) for _ in range(N_SLAB)]
    for t in range(SEG):
        rows = slice(t * SUBLANES, (t + 1) * SUBLANES)
        for c in range(N_SLAB):
            a, b = scan_inputs(rows, c, 0)
            hs[c] = a * hs[c] + b
            ps[c] = a * ps[c]
            hf_scr[rows, slab_lanes[c]] = hs[c]
            pf_ref[c, pl.ds(t, SUBLANES, stride=SEG_PITCH), :] = ps[c] * _segment_rows(gg_ref, c, t)
    for c in range(N_SLAB):
        ah_ref[0, :, slab_lanes[c]] = ps[c]
        ah_ref[1, :, slab_lanes[c]] = hs[c]

    hs = [jnp.zeros((SUBLANES, LANES), F32) for _ in range(N_SLAB)]
    ps = [jnp.ones((SUBLANES, LANES), F32) for _ in range(N_SLAB)]
    for t in reversed(range(SEG)):
        rows = slice(t * SUBLANES, (t + 1) * SUBLANES)
        for c in range(N_SLAB):
            a, b = scan_inputs(rows, c, 1)
            hs[c] = a * hs[c] + b
            ps[c] = a * ps[c]
            g = _segment_rows(gg_ref, c, t)
            yl_ref[c, pl.ds(t, SUBLANES, stride=SEG_PITCH), :] = (hf_scr[rows, slab_lanes[c]] + hs[c]) * g
            pb_ref[c, pl.ds(t, SUBLANES, stride=SEG_PITCH), :] = ps[c] * g
    for c in range(N_SLAB):
        ah_ref[2, :, slab_lanes[c]] = ps[c]
        ah_ref[3, :, slab_lanes[c]] = hs[c]


def _lru_local(ur, gg, conv_w, conv_b, wg, bg, lam):
    b, _, rows, _ = ur.shape
    n_tiles = rows // TILE_PITCH
    blocks_per_tile = TILE_PITCH // SUBLANES
    last_block = rows // SUBLANES - 1
    s = n_tiles * ROW_TILE
    slab = _slab_shape(b, s)
    slab_spec = _slab_spec()
    prev_spec = pl.BlockSpec((None, N_SLAB, SUBLANES, LANES),
                             lambda bi, i: (bi, 0, jnp.maximum(i * blocks_per_tile - 2, 0), 0))
    next_spec = pl.BlockSpec((None, N_SLAB, SUBLANES, LANES),
                             lambda bi, i: (bi, 0, jnp.minimum((i + 1) * blocks_per_tile, last_block), 0))
    n_seg = s // SEG
    n_half = D_LRU // MXU_K
    tile_f32 = pltpu.VMEM((ROW_TILE, D_LRU), F32)
    return pl.pallas_call(
        _lru_kernel,
        grid=(b, n_tiles),
        in_specs=[
            slab_spec, prev_spec, next_spec, slab_spec,
            _const_spec((CONV_WIDTH, D_LRU)),
            _const_spec((1, D_LRU)),
            _const_spec((n_half, MXU_K, GATE_COLS)),
            _const_spec((1, n_half * GATE_COLS)),
            _const_spec((2, D_LRU)),
        ],
        out_specs=[slab_spec, slab_spec, slab_spec,
                   pl.BlockSpec((None, 4, SUBLANES, D_LRU), lambda bi, i: (bi, 0, i, 0))],
        out_shape=[slab, slab, slab, jax.ShapeDtypeStruct((b, 4, n_seg, D_LRU), F32)],
        scratch_shapes=[tile_f32, pltpu.VMEM((ROW_TILE, n_half * GATE_COLS), F32), tile_f32],
        compiler_params=_params("parallel", "parallel"),
        name="lru_local",
    )(ur, ur, ur, gg, conv_w, conv_b.reshape(1, D_LRU), wg, bg, lam)


def _carry_kernel(ah_ref, cf_ref, cb_ref, *, n_seg):
    def fwd(sg, c):
        cf_ref[pl.ds(sg, 1), :] = c
        return ah_ref[1, pl.ds(sg, 1), :] + ah_ref[0, pl.ds(sg, 1), :] * c

    def bwd(n, c):
        sg = n_seg - 1 - n
        cb_ref[pl.ds(sg, 1), :] = c
        return ah_ref[3, pl.ds(sg, 1), :] + ah_ref[2, pl.ds(sg, 1), :] * c

    zero = jnp.zeros((1, D_LRU), F32)
    lax.fori_loop(0, n_seg, fwd, zero)
    lax.fori_loop(0, n_seg, bwd, zero)


def _segment_carries(ah):
    b, _, n_seg, _ = ah.shape
    c_shape = jax.ShapeDtypeStruct((b, n_seg, D_LRU), F32)
    c_spec = pl.BlockSpec((None, n_seg, D_LRU), lambda bi: (bi, 0, 0))
    return pl.pallas_call(
        functools.partial(_carry_kernel, n_seg=n_seg),
        grid=(b,),
        in_specs=[pl.BlockSpec((None, 4, n_seg, D_LRU), lambda bi: (bi, 0, 0, 0))],
        out_specs=[c_spec, c_spec],
        out_shape=[c_shape, c_shape],
        compiler_params=_params("parallel"),
        name="lru_carry",
    )(ah)


def _ffn_kernel(x_ref, yf_ref, yl_ref, pf_ref, pb_ref, cf_ref, cb_ref, mod_ref, g_ref,
                wo_ref, w1_ref, w2_ref, gfin_ref, o_ref, yr_scr, *, final_norm):
    for c in range(N_SLAB):
        lanes = slice(c * LANES, (c + 1) * LANES)
        for j in range(ROW_TILE // SEG):
            rows = slice(j * SEG_PITCH, j * SEG_PITCH + SEG)
            yr = (yl_ref[c, rows, :] + pf_ref[c, rows, :] * cf_ref[j:j + 1, lanes]
                  + pb_ref[c, rows, :] * cb_ref[j:j + 1, lanes])
            yr_scr[j * SEG:(j + 1) * SEG, lanes] = yr.astype(BF16)
    gate1 = mod_ref[:, 2 * D_MODEL:3 * D_MODEL]
    shift2 = mod_ref[:, 3 * D_MODEL:4 * D_MODEL]
    scale2 = mod_ref[:, 4 * D_MODEL:5 * D_MODEL]
    gate2 = mod_ref[:, 5 * D_MODEL:6 * D_MODEL]
    mix = (jnp.dot(yf_ref[...], wo_ref[0:D_FOURIER, :], preferred_element_type=F32)
           + jnp.dot(yr_scr[...], wo_ref[D_FOURIER:, :], preferred_element_type=F32))
    x1 = x_ref[...] + gate1 * mix
    h = (_rms_norm(x1, g_ref[...]) * (1.0 + scale2) + shift2).astype(BF16)
    acc = jnp.zeros((ROW_TILE, D_MODEL), F32)
    for k in range(D_FF // FF_CHUNK):
        cols = slice(k * FF_CHUNK, (k + 1) * FF_CHUNK)
        t = jnp.maximum(jnp.dot(h, w1_ref[:, cols], preferred_element_type=F32), 0.0)
        acc = acc + jnp.dot((t * t).astype(BF16), w2_ref[cols, :], preferred_element_type=F32)
    x2 = x1 + gate2 * acc
    if final_norm:
        x2 = _rms_norm(x2, gfin_ref[...])
    o_ref[...] = x2


def _out_and_mlp(x, yf, yl, pf, pb, cf, cb, mod, layer, row0, g_mlp, w_out, w_ff1, w_ff2, g_final,
                 final_norm):
    b, s, _ = x.shape
    segs = ROW_TILE // SEG
    x_spec = pl.BlockSpec((None, ROW_TILE, D_MODEL), lambda bi, i: (bi, i, 0))
    slab_spec = _slab_spec()
    c_spec = pl.BlockSpec((None, segs, D_LRU), lambda bi, i: (bi, i, 0))
    return pl.pallas_call(
        functools.partial(_ffn_kernel, final_norm=final_norm),
        grid=(b, s // ROW_TILE),
        in_specs=[
            x_spec,
            pl.BlockSpec((None, ROW_TILE, D_FOURIER), lambda bi, i: (bi, i, 0)),
            slab_spec, slab_spec, slab_spec, c_spec, c_spec,
            _mod_spec(layer, row0),
            _const_spec((1, D_MODEL)),
            _const_spec((D_MODEL, D_MODEL)),
            _const_spec((D_MODEL, D_FF)),
            _const_spec((D_FF, D_MODEL)),
            _const_spec((1, D_MODEL)),
        ],
        out_specs=x_spec,
        out_shape=jax.ShapeDtypeStruct(x.shape, F32),
        scratch_shapes=[pltpu.VMEM((ROW_TILE, D_LRU), BF16)],
        compiler_params=_params("parallel", "parallel"),
        name="outproj_mlp",
    )(x, yf, yl, pf, pb, cf, cb, mod, g_mlp.reshape(1, D_MODEL), w_out, w_ff1, w_ff2,
      g_final.reshape(1, D_MODEL))


def _gate_weights(w_rgate, b_rgate, w_igate, b_igate):
    depth = w_rgate.shape[0]
    n_half = D_LRU // MXU_K
    heads_per_half = N_HEADS // n_half
    w_halves, b_halves = [], []
    for hh in range(n_half):
        hs = slice(hh * heads_per_half, (hh + 1) * heads_per_half)
        w_cols, b_cols = [], []
        for d in range(2):
            for w, bias in ((w_rgate, b_rgate), (w_igate, b_igate)):
                w_cols.append(_block_diag(w[:, d, hs]))
                b_cols.append(bias[:, d, hs].reshape(depth, MXU_K))
        w_halves.append(jnp.concatenate(w_cols, axis=-1))
        b_halves.append(jnp.concatenate(b_cols, axis=-1))
    wg = jnp.stack(w_halves, axis=1).astype(BF16)
    bg = (0.5 * jnp.concatenate(b_halves, axis=-1)).reshape(depth, 1, n_half * GATE_COLS)
    return wg, bg


def _trunk(x, mod, row0, layers, g_final):
    depth = len(layers)
    for l, p in enumerate(layers):
        ua, uv, ur, gg = _in_projection(x, mod, l, row0, p["g_mix"], p["w_fold"])
        yf = _fourier_mix(ua, uv, x.shape[1])
        yl, pf, pb, ah = _lru_local(ur, gg, p["conv_w"], p["conv_b"], p["wg"], p["bg"], p["lam"])
        cf, cb = _segment_carries(ah)
        x = _out_and_mlp(x, yf, yl, pf, pb, cf, cb, mod, l, row0, p["g_mlp"], p["w_out"],
                         p["w_ff1"], p["w_ff2"], g_final, final_norm=(l == depth - 1))
    return x


def kernel(x_prompt, x_sample, c_prompt, c_sample, g_mix, g_mlp, w_mod, b_mod, w_in, w_fourier,
           conv_w, conv_b, w_rgate, b_rgate, w_igate, b_igate, lru_lambda, w_out, w_ff1, w_ff2,
           g_final):
    depth = w_in.shape[0]
    n_prompt = c_prompt.shape[0]
    n_cond = n_prompt + c_sample.shape[0]
    pad = -n_cond % SUBLANES
    c_all = jnp.concatenate([c_prompt, c_sample, jnp.zeros((pad, D_MODEL), F32)], axis=0)
    mod = _modulation(c_all, w_mod, b_mod)
    mod = mod.reshape(depth, n_cond + pad, 1, N_MOD * D_MODEL)

    w_fold = _fold_in_weights(w_in, w_fourier)
    wg, bg = _gate_weights(w_rgate, b_rgate, w_igate, b_igate)
    w_out_b = w_out.astype(BF16)
    w_ff1_b = w_ff1.astype(BF16)
    w_ff2_b = w_ff2.astype(BF16)
    layers = [dict(g_mix=g_mix[l], g_mlp=g_mlp[l], w_fold=w_fold[l], conv_w=conv_w[l], conv_b=conv_b[l],
                   wg=wg[l], bg=bg[l], lam=lru_lambda[l], w_out=w_out_b[l], w_ff1=w_ff1_b[l],
                   w_ff2=w_ff2_b[l]) for l in range(depth)]
    y_prompt = _trunk(x_prompt, mod, 0, layers, g_final)
    y_sample = _trunk(x_sample, mod, n_prompt, layers, g_final)
    return (y_prompt, y_sample)
```

```python
import functools
import math

import numpy as np
import jax
import jax.numpy as jnp
from jax import lax
from jax.experimental import pallas as pl
from jax.experimental.pallas import tpu as pltpu

F32 = jnp.float32
BF16 = jnp.bfloat16
HIGHEST = lax.Precision.HIGHEST

D_MODEL = 1024
HEAD_DIM = 64
D_FOURIER = 512
D_LRU = 512
N_HEADS = 8
D_FF = 4096
N_MOD = 6
EPS = 1e-6
LRU_C = 8.0
CONV_WIDTH = 4

LANES = 128
SUBLANES = 8
BF16_ROWS = 16
N_SLAB = D_LRU // LANES
ROW_TILE = 512
SEG = ROW_TILE // SUBLANES
SEG_PITCH = SEG + SUBLANES
TILE_PITCH = SUBLANES * SEG_PITCH
LOG2E = 1.4426950408889634
MXU_K = 256
GATE_COLS = 4 * MXU_K
DFT_N2 = 256
DFT_GROUP = BF16_ROWS
DFT_ROWS_PER_STEP = 1024
FF_CHUNK = 1024
MOD_COLS = 3072
VMEM_LIMIT = 56 * 1024 * 1024


def _const_spec(shape):
    n = len(shape)
    return pl.BlockSpec(shape, lambda *_: (0,) * n, pipeline_mode=pl.Buffered(1))


def _params(*sem):
    return pltpu.CompilerParams(dimension_semantics=sem, vmem_limit_bytes=VMEM_LIMIT)


def _mod_kernel(c_ref, w_ref, b_ref, o_ref):
    c = c_ref[...]
    s = c * jax.nn.sigmoid(c)
    o_ref[...] = jnp.dot(s, w_ref[...], precision=HIGHEST, preferred_element_type=F32) + b_ref[...]


def _modulation(c_all, w_mod, b_mod):
    depth = w_mod.shape[0]
    n_out = N_MOD * D_MODEL
    rows = c_all.shape[0]
    return pl.pallas_call(
        _mod_kernel,
        grid=(depth, n_out // MOD_COLS),
        in_specs=[
            pl.BlockSpec((rows, D_MODEL), lambda l, j: (0, 0)),
            pl.BlockSpec((None, D_MODEL, MOD_COLS), lambda l, j: (l, 0, j)),
            pl.BlockSpec((None, 1, MOD_COLS), lambda l, j: (l, 0, j)),
        ],
        out_specs=pl.BlockSpec((None, rows, MOD_COLS), lambda l, j: (l, 0, j)),
        out_shape=jax.ShapeDtypeStruct((depth, rows, n_out), F32),
        compiler_params=_params("parallel", "parallel"),
        name="adaln_mod",
    )(c_all, w_mod, b_mod.reshape(depth, 1, n_out))


def _prep_kernel(win_ref, bdwf_ref, bdc_ref, bds_ref, o_ref):
    wf = bdwf_ref[...]
    ma = jnp.dot(bdc_ref[...], wf, precision=HIGHEST, preferred_element_type=F32)
    mv = -jnp.dot(bds_ref[...], wf, precision=HIGHEST, preferred_element_type=F32)
    wi = win_ref[:, 0:D_FOURIER]
    o_ref[:, 0:D_FOURIER] = jnp.dot(wi, ma, precision=HIGHEST, preferred_element_type=F32).astype(BF16)
    o_ref[:, D_FOURIER:2 * D_FOURIER] = jnp.dot(
        wi, mv, precision=HIGHEST, preferred_element_type=F32).astype(BF16)
    o_ref[:, 2 * D_FOURIER:] = win_ref[:, D_FOURIER:].astype(BF16)


def _block_diag(w):
    h, d, e = w.shape[-3:]
    eye = jnp.eye(h, dtype=w.dtype)
    out = w[..., :, :, None, :] * eye[:, None, :, None]
    return out.reshape(w.shape[:-3] + (h * d, h * e))


def _channel_dft_tables():
    k = np.arange(HEAD_DIM)
    ang = 2.0 * np.pi * np.outer(k, k) / HEAD_DIM
    eye = np.eye(N_HEADS)
    return (jnp.asarray(np.kron(eye, np.cos(ang)), F32), jnp.asarray(np.kron(eye, np.sin(ang)), F32))


def _fold_in_weights(w_in, w_fourier):
    depth = w_in.shape[0]
    d_in = w_in.shape[2]
    n_out = d_in + D_FOURIER
    bdc, bds = _channel_dft_tables()
    bdwf = _block_diag(w_fourier)
    return pl.pallas_call(
        _prep_kernel,
        grid=(depth,),
        in_specs=[
            pl.BlockSpec((None, D_MODEL, d_in), lambda l: (l, 0, 0)),
            pl.BlockSpec((None, D_FOURIER, D_FOURIER), lambda l: (l, 0, 0)),
            _const_spec((D_FOURIER, D_FOURIER)),
            _const_spec((D_FOURIER, D_FOURIER)),
        ],
        out_specs=pl.BlockSpec((None, D_MODEL, n_out), lambda l: (l, 0, 0)),
        out_shape=jax.ShapeDtypeStruct((depth, D_MODEL, n_out), BF16),
        compiler_params=_params("parallel"),
        name="fold_in_weights",
    )(w_in, bdwf, bdc, bds)


def _rms_norm(x, g):
    ms = jnp.mean(x * x, axis=-1, keepdims=True)
    return x * lax.rsqrt(ms + EPS) * g


def _gelu_tanh(x):
    c = math.sqrt(2.0 / math.pi)
    return 0.5 * x * (1.0 + jnp.tanh(c * (x + 0.044715 * (x * x * x))))


def _mod_spec(layer, row0):
    return pl.BlockSpec((None, None, 1, N_MOD * D_MODEL), lambda b, i: (layer, row0 + b, 0, 0))


def _slab_shape(b, s):
    return jax.ShapeDtypeStruct((b, N_SLAB, s // ROW_TILE * TILE_PITCH, LANES), F32)


def _slab_spec():
    return pl.BlockSpec((None, N_SLAB, TILE_PITCH, LANES), lambda bi, i: (bi, 0, i, 0))


def _inproj_kernel(x_ref, mod_ref, g_ref, w_ref, ua_ref, uv_ref, ur_ref, gg_ref, *, n1):
    shift = mod_ref[:, 0:D_MODEL]
    scale = mod_ref[:, D_MODEL:2 * D_MODEL]
    h = _rms_norm(x_ref[...], g_ref[...] * (1.0 + scale)) + shift
    z = jnp.dot(h.astype(BF16), w_ref[...], preferred_element_type=F32)
    n_grp = n1 // SUBLANES
    for c in range(N_SLAB):
        for s2 in range(ROW_TILE // n1):
            for grp in range(n_grp):
                src = slice(s2 * n1 + grp * SUBLANES, s2 * n1 + (grp + 1) * SUBLANES)
                dst = slice(s2 * SUBLANES, (s2 + 1) * SUBLANES)
                ua_ref[c, grp, dst, :] = z[src, c * LANES:(c + 1) * LANES]
                uv_ref[c, grp, dst, :] = z[src, D_FOURIER + c * LANES:D_FOURIER + (c + 1) * LANES]
    r0 = 2 * D_FOURIER
    g0 = r0 + D_LRU
    pad = jnp.zeros((SEG_PITCH - SEG, LANES), F32)
    for c in range(N_SLAB):
        ur = z[:, r0 + c * LANES:r0 + (c + 1) * LANES]
        for j in range(SUBLANES):
            ur_ref[c, j * SEG_PITCH:j * SEG_PITCH + SEG, :] = ur[j * SEG:(j + 1) * SEG]
            ur_ref[c, j * SEG_PITCH + SEG:(j + 1) * SEG_PITCH, :] = pad
    gg_ref[...] = _gelu_tanh(z[:, g0:]).astype(BF16)


def _in_projection(x, mod, layer, row0, g_mix, w_fold):
    b, s, _ = x.shape
    n_out = w_fold.shape[-1]
    slab = _slab_shape(b, s)
    slab_spec = _slab_spec()
    n1 = s // DFT_N2
    n_grp = n1 // SUBLANES
    grp_rows = ROW_TILE // n1 * SUBLANES
    half = jax.ShapeDtypeStruct((b, N_SLAB, n_grp, DFT_N2 * SUBLANES, LANES), F32)
    half_spec = pl.BlockSpec((None, N_SLAB, n_grp, grp_rows, LANES), lambda bi, i: (bi, 0, 0, i, 0))
    return pl.pallas_call(
        functools.partial(_inproj_kernel, n1=n1),
        grid=(b, s // ROW_TILE),
        in_specs=[
            pl.BlockSpec((None, ROW_TILE, D_MODEL), lambda bi, i: (bi, i, 0)),
            _mod_spec(layer, row0),
            _const_spec((1, D_MODEL)),
            _const_spec((D_MODEL, n_out)),
        ],
        out_specs=[half_spec, half_spec, slab_spec,
                   pl.BlockSpec((None, ROW_TILE, D_LRU), lambda bi, i: (bi, i, 0))],
        out_shape=[half, half, slab, jax.ShapeDtypeStruct((b, s, D_LRU), BF16)],
        compiler_params=_params("parallel", "parallel"),
        name="inproj",
    )(x, mod, g_mix.reshape(1, D_MODEL), w_fold)


def _dft_tables(seq):
    n2 = DFT_N2
    n1 = seq // n2
    k2 = np.arange(n2)
    ang = 2.0 * np.pi * np.outer(k2, k2) / n2
    c2 = np.cos(ang) / math.sqrt(n2)
    s2 = np.sin(ang) / math.sqrt(n2)
    fa = np.concatenate([c2, -s2], axis=0)
    fv = np.concatenate([s2, c2], axis=0)
    g = DFT_GROUP
    n_groups = n2 // g
    rows = n1 * g
    k1 = np.arange(n1)
    s1 = np.arange(n1)
    kp = np.arange(g)
    scale = 1.0 / math.sqrt(n1 * HEAD_DIM)
    lc = np.zeros((n_groups, n1, g, n1, g), np.float64)
    ls = np.zeros((n_groups, n1, g, n1, g), np.float64)
    for grp in range(n_groups):
        k = n2 * k1[:, None] + (g * grp + kp)[None, :]
        ang_b = 2.0 * np.pi * (k[:, :, None] * s1[None, None, :] % seq) / seq
        for q in range(g):
            lc[grp, :, q, :, q] = np.cos(ang_b[:, q, :]) * scale
            ls[grp, :, q, :, q] = np.sin(ang_b[:, q, :]) * scale
    lc = lc.reshape(n_groups, rows, rows)
    ls = ls.reshape(n_groups, rows, rows)
    as_bf16 = lambda a: jnp.asarray(a.astype(np.float32)).astype(BF16)
    return as_bf16(fa), as_bf16(fv), as_bf16(lc), as_bf16(ls)


def _dft_kernel(a_ref, v_ref, fa_ref, fv_ref, lc_ref, ls_ref, y_ref, tre_scr, tim_scr, *, n1, groups_per_step):
    k = pl.program_id(1)
    n_grp = n1 // SUBLANES

    @pl.when(k < n_grp)
    def _():
        for q in range(SUBLANES):
            rows = pl.ds(q, DFT_N2, stride=SUBLANES)
            a = jnp.concatenate([a_ref[c, rows, :] for c in range(N_SLAB)], axis=1).astype(BF16)
            v = jnp.concatenate([v_ref[c, rows, :] for c in range(N_SLAB)], axis=1).astype(BF16)
            o = (jnp.dot(fa_ref[...], a, preferred_element_type=F32)
                 + jnp.dot(fv_ref[...], v, preferred_element_type=F32))
            tre_scr[k * SUBLANES + q] = o[0:DFT_N2].astype(BF16)
            tim_scr[k * SUBLANES + q] = o[DFT_N2:].astype(BF16)

    @pl.when(k >= n_grp)
    def _():
        rows = n1 * DFT_GROUP
        for j in range(groups_per_step):
            first = pl.multiple_of(((k - n_grp) * groups_per_step + j) * DFT_GROUP, DFT_GROUP)
            tr = tre_scr[:, pl.ds(first, DFT_GROUP), :].reshape(rows, D_FOURIER)
            ti = tim_scr[:, pl.ds(first, DFT_GROUP), :].reshape(rows, D_FOURIER)
            o = (jnp.dot(lc_ref[j], tr, preferred_element_type=F32)
                 + jnp.dot(ls_ref[j], ti, preferred_element_type=F32))
            y_ref[:, j * DFT_GROUP:(j + 1) * DFT_GROUP, :] = o.astype(BF16).reshape(n1, DFT_GROUP, D_FOURIER)


def _fourier_mix(ua, uv, s):
    b = ua.shape[0]
    n2 = DFT_N2
    n1 = s // n2
    n_grp = n1 // SUBLANES
    rows = n1 * DFT_GROUP
    groups_per_step = DFT_ROWS_PER_STEP // rows
    n_out = n2 // (DFT_GROUP * groups_per_step)
    fa, fv, lc, ls = _dft_tables(s)
    in_spec = pl.BlockSpec((None, N_SLAB, None, n2 * SUBLANES, LANES),
                           lambda bi, k: (bi, 0, jnp.minimum(k, n_grp - 1), 0, 0))
    l_spec = pl.BlockSpec((groups_per_step, rows, rows), lambda bi, k: (jnp.maximum(k - n_grp, 0), 0, 0))
    t_scr = pltpu.VMEM((n1, n2, D_FOURIER), BF16)
    y = pl.pallas_call(
        functools.partial(_dft_kernel, n1=n1, groups_per_step=groups_per_step),
        grid=(b, n_grp + n_out),
        in_specs=[in_spec, in_spec, _const_spec((2 * n2, n2)), _const_spec((2 * n2, n2)), l_spec, l_spec],
        out_specs=pl.BlockSpec((None, n1, DFT_GROUP * groups_per_step, D_FOURIER),
                               lambda bi, k: (bi, 0, jnp.maximum(k - n_grp, 0), 0)),
        out_shape=jax.ShapeDtypeStruct((b, n1, n2, D_FOURIER), BF16),
        scratch_shapes=[t_scr, t_scr],
        compiler_params=_params("parallel", "arbitrary"),
        name="seq_dft",
    )(ua, uv, fa, fv, lc, ls)
    return y.reshape(b, s, D_FOURIER)


def _segment_rows(ref, c, t):
    return ref[c, pl.ds(t, SUBLANES, stride=SEG_PITCH), :]


def _lru_kernel(ur_ref, prev_ref, next_ref, cw_ref, cb_ref, wg_ref, bg_ref, lam_ref,
                yl_ref, pf_ref, pb_ref, ah_ref, uc_scr, pre_scr, hf_scr):
    i = pl.program_id(1)
    n_tiles = pl.num_programs(1)
    sub = lax.broadcasted_iota(jnp.int32, (SUBLANES, LANES), 0)
    slab_lanes = [slice(c * LANES, (c + 1) * LANES) for c in range(N_SLAB)]

    pad = jnp.zeros((SEG_PITCH - SEG, LANES), F32)
    for c in range(N_SLAB):
        for j in range(SUBLANES):
            pad_rows = slice(j * SEG_PITCH + SEG, (j + 1) * SEG_PITCH)
            yl_ref[c, pad_rows, :] = pad
            pf_ref[c, pad_rows, :] = pad
            pb_ref[c, pad_rows, :] = pad

    for c in range(N_SLAB):
        lanes = slab_lanes[c]
        prev_rows = [jnp.where(i > 0, prev_ref[c, SUBLANES - 2 + q:SUBLANES - 1 + q, :], 0.0) for q in range(2)]
        next_row = jnp.where(i < n_tiles - 1, next_ref[c, 0:1, :], 0.0)
        first = _segment_rows(ur_ref, c, 0)
        window = [jnp.where(sub == 0, prev_rows[q], pltpu.roll(_segment_rows(ur_ref, c, SEG - 2 + q), 1, 0))
                  for q in range(2)] + [first]
        cw = [jnp.broadcast_to(0.5 * cw_ref[k:k + 1, lanes], (SUBLANES, LANES)) for k in range(CONV_WIDTH)]
        cbias = jnp.broadcast_to(0.5 * cb_ref[:, lanes], (SUBLANES, LANES))
        for t in range(SEG):
            if t + 1 < SEG:
                nxt = _segment_rows(ur_ref, c, t + 1)
            else:
                nxt = jnp.where(sub == SUBLANES - 1, next_row, pltpu.roll(first, SUBLANES - 1, 0))
            window.append(nxt)
            uc = cbias + window[0] * cw[0] + window[1] * cw[1] + window[2] * cw[2] + window[3] * cw[3]
            uc_scr[t * SUBLANES:(t + 1) * SUBLANES, lanes] = uc
            window.pop(0)

    for hh in range(D_LRU // MXU_K):
        cols = slice(hh * GATE_COLS, (hh + 1) * GATE_COLS)
        lhs = uc_scr[:, hh * MXU_K:(hh + 1) * MXU_K].astype(BF16)
        pre_scr[:, cols] = jnp.dot(lhs, wg_ref[hh], preferred_element_type=F32) + bg_ref[:, cols]

    neg_lam = -lam_ref[...]
    softplus = jnp.maximum(neg_lam, 0.0) + jnp.log1p(jnp.exp(-jnp.abs(neg_lam)))
    k_all = (-0.5 * LRU_C * LOG2E) * softplus
    k_rows = [[jnp.broadcast_to(k_all[d:d + 1, slab_lanes[c]], (SUBLANES, LANES)) for c in range(N_SLAB)]
              for d in range(2)]

    def scan_inputs(rows, c, d):
        col = (c * LANES // MXU_K) * GATE_COLS + d * 2 * MXU_K + (c * LANES) % MXU_K
        th_r = jnp.tanh(pre_scr[rows, col:col + LANES])
        th_i = jnp.tanh(pre_scr[rows, col + MXU_K:col + MXU_K + LANES])
        k = k_rows[d][c]
        a = jnp.exp2(k * th_r + k)
        m = 1.0 - a * a
        mult = jnp.where(m > 0.0, m * lax.rsqrt(m), 0.0)
        return a, mult * ((th_i + 1.0) * uc_scr[rows, slab_lanes[c]])

    hs = [jnp.zeros((SUBLANES, LANES), F32) for _ in range(N_SLAB)]
    ps = [jnp.ones((SUBLANES, LANES), F32) for _ in range(N_SLAB)]
    for t in range(SEG):
        rows = slice(t * SUBLANES, (t + 1) * SUBLANES)
        for c in range(N_SLAB):
            a, b = scan_inputs(rows, c, 0)
            hs[c] = a * hs[c] + b
            ps[c] = a * ps[c]
            hf_scr[rows, slab_lanes[c]] = hs[c]
            pf_ref[c, pl.ds(t, SUBLANES, stride=SEG_PITCH), :] = ps[c]
    for c in range(N_SLAB):
        ah_ref[0, :, slab_lanes[c]] = ps[c]
        ah_ref[1, :, slab_lanes[c]] = hs[c]

    hs = [jnp.zeros((SUBLANES, LANES), F32) for _ in range(N_SLAB)]
    ps = [jnp.ones((SUBLANES, LANES), F32) for _ in range(N_SLAB)]
    for t in reversed(range(SEG)):
        rows = slice(t * SUBLANES, (t + 1) * SUBLANES)
        for c in range(N_SLAB):
            a, b = scan_inputs(rows, c, 1)
            hs[c] = a * hs[c] + b
            ps[c] = a * ps[c]
            yl_ref[c, pl.ds(t, SUBLANES, stride=SEG_PITCH), :] = hf_scr[rows, slab_lanes[c]] + hs[c]
            pb_ref[c, pl.ds(t, SUBLANES, stride=SEG_PITCH), :] = ps[c]
    for c in range(N_SLAB):
        ah_ref[2, :, slab_lanes[c]] = ps[c]
        ah_ref[3, :, slab_lanes[c]] = hs[c]


def _lru_local(ur, conv_w, conv_b, wg, bg, lam):
    b, _, rows, _ = ur.shape
    n_tiles = rows // TILE_PITCH
    blocks_per_tile = TILE_PITCH // SUBLANES
    last_block = rows // SUBLANES - 1
    s = n_tiles * ROW_TILE
    slab = _slab_shape(b, s)
    slab_spec = _slab_spec()
    prev_spec = pl.BlockSpec((None, N_SLAB, SUBLANES, LANES),
                             lambda bi, i: (bi, 0, jnp.maximum(i * blocks_per_tile - 2, 0), 0))
    next_spec = pl.BlockSpec((None, N_SLAB, SUBLANES, LANES),
                             lambda bi, i: (bi, 0, jnp.minimum((i + 1) * blocks_per_tile, last_block), 0))
    n_seg = s // SEG
    n_half = D_LRU // MXU_K
    tile_f32 = pltpu.VMEM((ROW_TILE, D_LRU), F32)
    return pl.pallas_call(
        _lru_kernel,
        grid=(b, n_tiles),
        in_specs=[
            slab_spec, prev_spec, next_spec,
            _const_spec((CONV_WIDTH, D_LRU)),
            _const_spec((1, D_LRU)),
            _const_spec((n_half, MXU_K, GATE_COLS)),
            _const_spec((1, n_half * GATE_COLS)),
            _const_spec((2, D_LRU)),
        ],
        out_specs=[slab_spec, slab_spec, slab_spec,
                   pl.BlockSpec((None, 4, SUBLANES, D_LRU), lambda bi, i: (bi, 0, i, 0))],
        out_shape=[slab, slab, slab, jax.ShapeDtypeStruct((b, 4, n_seg, D_LRU), F32)],
        scratch_shapes=[tile_f32, pltpu.VMEM((ROW_TILE, n_half * GATE_COLS), F32), tile_f32],
        compiler_params=_params("parallel", "parallel"),
        name="lru_local",
    )(ur, ur, ur, conv_w, conv_b.reshape(1, D_LRU), wg, bg, lam)


def _carry_kernel(ah_ref, cf_ref, cb_ref, *, n_seg):
    def fwd(sg, c):
        cf_ref[pl.ds(sg, 1), :] = c
        return ah_ref[1, pl.ds(sg, 1), :] + ah_ref[0, pl.ds(sg, 1), :] * c

    def bwd(n, c):
        sg = n_seg - 1 - n
        cb_ref[pl.ds(sg, 1), :] = c
        return ah_ref[3, pl.ds(sg, 1), :] + ah_ref[2, pl.ds(sg, 1), :] * c

    zero = jnp.zeros((1, D_LRU), F32)
    lax.fori_loop(0, n_seg, fwd, zero)
    lax.fori_loop(0, n_seg, bwd, zero)


def _segment_carries(ah):
    b, _, n_seg, _ = ah.shape
    c_shape = jax.ShapeDtypeStruct((b, n_seg, D_LRU), F32)
    c_spec = pl.BlockSpec((None, n_seg, D_LRU), lambda bi: (bi, 0, 0))
    return pl.pallas_call(
        functools.partial(_carry_kernel, n_seg=n_seg),
        grid=(b,),
        in_specs=[pl.BlockSpec((None, 4, n_seg, D_LRU), lambda bi: (bi, 0, 0, 0))],
        out_specs=[c_spec, c_spec],
        out_shape=[c_shape, c_shape],
        compiler_params=_params("parallel"),
        name="lru_carry",
    )(ah)


def _ffn_kernel(x_ref, yf_ref, gg_ref, yl_ref, pf_ref, pb_ref, cf_ref, cb_ref, mod_ref, g_ref,
                wo_ref, w1_ref, w2_ref, gfin_ref, o_ref, yr_scr, *, final_norm):
    for c in range(N_SLAB):
        lanes = slice(c * LANES, (c + 1) * LANES)
        for j in range(ROW_TILE // SEG):
            rows = slice(j * SEG_PITCH, j * SEG_PITCH + SEG)
            hsum = (yl_ref[c, rows, :] + pf_ref[c, rows, :] * cf_ref[j:j + 1, lanes]
                    + pb_ref[c, rows, :] * cb_ref[j:j + 1, lanes])
            gate = gg_ref[j * SEG:(j + 1) * SEG, lanes].astype(F32)
            yr_scr[j * SEG:(j + 1) * SEG, lanes] = (hsum * gate).astype(BF16)
    gate1 = mod_ref[:, 2 * D_MODEL:3 * D_MODEL]
    shift2 = mod_ref[:, 3 * D_MODEL:4 * D_MODEL]
    scale2 = mod_ref[:, 4 * D_MODEL:5 * D_MODEL]
    gate2 = mod_ref[:, 5 * D_MODEL:6 * D_MODEL]
    mix = (jnp.dot(yf_ref[...], wo_ref[0:D_FOURIER, :], preferred_element_type=F32)
           + jnp.dot(yr_scr[...], wo_ref[D_FOURIER:, :], preferred_element_type=F32))
    x1 = x_ref[...] + gate1 * mix
    h = (_rms_norm(x1, g_ref[...] * (1.0 + scale2)) + shift2).astype(BF16)
    acc = jnp.zeros((ROW_TILE, D_MODEL), F32)
    for k in range(D_FF // FF_CHUNK):
        cols = slice(k * FF_CHUNK, (k + 1) * FF_CHUNK)
        t = jnp.maximum(jnp.dot(h, w1_ref[:, cols], preferred_element_type=F32), 0.0)
        acc = acc + jnp.dot((t * t).astype(BF16), w2_ref[cols, :], preferred_element_type=F32)
    x2 = x1 + gate2 * acc
    if final_norm:
        x2 = _rms_norm(x2, gfin_ref[...])
    o_ref[...] = x2


def _out_and_mlp(x, yf, gg, yl, pf, pb, cf, cb, mod, layer, row0, g_mlp, w_out, w_ff1, w_ff2, g_final,
                 final_norm):
    b, s, _ = x.shape
    segs = ROW_TILE // SEG
    x_spec = pl.BlockSpec((None, ROW_TILE, D_MODEL), lambda bi, i: (bi, i, 0))
    slab_spec = _slab_spec()
    c_spec = pl.BlockSpec((None, segs, D_LRU), lambda bi, i: (bi, i, 0))
    return pl.pallas_call(
        functools.partial(_ffn_kernel, final_norm=final_norm),
        grid=(b, s // ROW_TILE),
        in_specs=[
            x_spec,
            pl.BlockSpec((None, ROW_TILE, D_FOURIER), lambda bi, i: (bi, i, 0)),
            pl.BlockSpec((None, ROW_TILE, D_LRU), lambda bi, i: (bi, i, 0)),
            slab_spec, slab_spec, slab_spec, c_spec, c_spec,
            _mod_spec(layer, row0),
            _const_spec((1, D_MODEL)),
            _const_spec((D_MODEL, D_MODEL)),
            _const_spec((D_MODEL, D_FF)),
            _const_spec((D_FF, D_MODEL)),
            _const_spec((1, D_MODEL)),
        ],
        out_specs=x_spec,
        out_shape=jax.ShapeDtypeStruct(x.shape, F32),
        scratch_shapes=[pltpu.VMEM((ROW_TILE, D_LRU), BF16)],
        compiler_params=_params("parallel", "parallel"),
        name="outproj_mlp",
    )(x, yf, gg, yl, pf, pb, cf, cb, mod, g_mlp.reshape(1, D_MODEL), w_out, w_ff1, w_ff2,
      g_final.reshape(1, D_MODEL))


def _gate_weights(w_rgate, b_rgate, w_igate, b_igate):
    depth = w_rgate.shape[0]
    n_half = D_LRU // MXU_K
    heads_per_half = N_HEADS // n_half
    w_halves, b_halves = [], []
    for hh in range(n_half):
        hs = slice(hh * heads_per_half, (hh + 1) * heads_per_half)
        w_cols, b_cols = [], []
        for d in range(2):
            for w, bias in ((w_rgate, b_rgate), (w_igate, b_igate)):
                w_cols.append(_block_diag(w[:, d, hs]))
                b_cols.append(bias[:, d, hs].reshape(depth, MXU_K))
        w_halves.append(jnp.concatenate(w_cols, axis=-1))
        b_halves.append(jnp.concatenate(b_cols, axis=-1))
    wg = jnp.stack(w_halves, axis=1).astype(BF16)
    bg = (0.5 * jnp.concatenate(b_halves, axis=-1)).reshape(depth, 1, n_half * GATE_COLS)
    return wg, bg


def _trunk(x, mod, row0, layers, g_final):
    depth = len(layers)
    for l, p in enumerate(layers):
        ua, uv, ur, gg = _in_projection(x, mod, l, row0, p["g_mix"], p["w_fold"])
        yf = _fourier_mix(ua, uv, x.shape[1])
        yl, pf, pb, ah = _lru_local(ur, p["conv_w"], p["conv_b"], p["wg"], p["bg"], p["lam"])
        cf, cb = _segment_carries(ah)
        x = _out_and_mlp(x, yf, gg, yl, pf, pb, cf, cb, mod, l, row0, p["g_mlp"], p["w_out"],
                         p["w_ff1"], p["w_ff2"], g_final, final_norm=(l == depth - 1))
    return x


def kernel(x_prompt, x_sample, c_prompt, c_sample, g_mix, g_mlp, w_mod, b_mod, w_in, w_fourier,
           conv_w, conv_b, w_rgate, b_rgate, w_igate, b_igate, lru_lambda, w_out, w_ff1, w_ff2,
           g_final):
    depth = w_in.shape[0]
    n_prompt = c_prompt.shape[0]
    n_cond = n_prompt + c_sample.shape[0]
    pad = -n_cond % SUBLANES
    c_all = jnp.concatenate([c_prompt, c_sample, jnp.zeros((pad, D_MODEL), F32)], axis=0)
    mod = _modulation(c_all, w_mod, b_mod)
    mod = mod.reshape(depth, n_cond + pad, 1, N_MOD * D_MODEL)

    w_fold = _fold_in_weights(w_in, w_fourier)
    wg, bg = _gate_weights(w_rgate, b_rgate, w_igate, b_igate)
    w_out_b = w_out.astype(BF16)
    w_ff1_b = w_ff1.astype(BF16)
    w_ff2_b = w_ff2.astype(BF16)
    layers = [dict(g_mix=g_mix[l], g_mlp=g_mlp[l], w_fold=w_fold[l], conv_w=conv_w[l], conv_b=conv_b[l],
                   wg=wg[l], bg=bg[l], lam=lru_lambda[l], w_out=w_out_b[l], w_ff1=w_ff1_b[l],
                   w_ff2=w_ff2_b[l]) for l in range(depth)]
    y_prompt = _trunk(x_prompt, mod, 0, layers, g_final)
    y_sample = _trunk(x_sample, mod, n_prompt, layers, g_final)
    return (y_prompt, y_sample)
```

```python
import functools
import math

import numpy as np
import jax
import jax.numpy as jnp
from jax import lax
from jax.experimental import pallas as pl
from jax.experimental.pallas import tpu as pltpu

F32 = jnp.float32
BF16 = jnp.bfloat16

D_MODEL = 1024
HEAD_DIM = 64
D_FOURIER = 512
D_LRU = 512
N_HEADS = 8
D_FF = 4096
N_MOD = 6
EPS = 1e-6
LRU_C = 8.0
CONV_WIDTH = 4

LANES = 128
SUBLANES = 8
BF16_ROWS = 16
N_SLAB = D_LRU // LANES
ROW_TILE = 512
SEG = ROW_TILE // SUBLANES
SEG_PITCH = SEG + SUBLANES
TILE_PITCH = SUBLANES * SEG_PITCH
LOG2E = 1.4426950408889634
MXU_K = 256
GATE_COLS = 4 * MXU_K
DFT_N2 = 256
DFT_GROUP = BF16_ROWS
DFT_ROWS_PER_STEP = 1024
CARRY_UNROLL = 4
FF_CHUNK = 1024
MOD_COLS = 1536
VMEM_LIMIT = 56 * 1024 * 1024


def _const_spec(shape):
    n = len(shape)
    return pl.BlockSpec(shape, lambda *_: (0,) * n, pipeline_mode=pl.Buffered(1))


def _layer_spec(layer, shape):
    n = len(shape)
    return pl.BlockSpec((None,) + tuple(shape), lambda *_: (layer,) + (0,) * n, pipeline_mode=pl.Buffered(1))


def _dot3(a, b):
    a_hi = a.astype(BF16)
    b_hi = b.astype(BF16)
    a_lo = (a - a_hi.astype(F32)).astype(BF16)
    b_lo = (b - b_hi.astype(F32)).astype(BF16)
    return (jnp.dot(a_hi, b_hi, preferred_element_type=F32) + jnp.dot(a_hi, b_lo, preferred_element_type=F32)
            + jnp.dot(a_lo, b_hi, preferred_element_type=F32))


def _params(*sem):
    return pltpu.CompilerParams(dimension_semantics=sem, vmem_limit_bytes=VMEM_LIMIT)


def _mod_kernel(c_ref, w_ref, b_ref, o_ref):
    c = c_ref[...]
    s = c * jax.nn.sigmoid(c)
    o_ref[...] = _dot3(s, w_ref[...]) + b_ref[...]


def _modulation(c_all, w_mod, b_mod):
    depth = w_mod.shape[0]
    n_out = N_MOD * D_MODEL
    rows = c_all.shape[0]
    return pl.pallas_call(
        _mod_kernel,
        grid=(depth, n_out // MOD_COLS),
        in_specs=[
            pl.BlockSpec((rows, D_MODEL), lambda l, j: (0, 0)),
            pl.BlockSpec((None, D_MODEL, MOD_COLS), lambda l, j: (l, 0, j)),
            pl.BlockSpec((None, 1, MOD_COLS), lambda l, j: (l, 0, j)),
        ],
        out_specs=pl.BlockSpec((None, rows, MOD_COLS), lambda l, j: (l, 0, j)),
        out_shape=jax.ShapeDtypeStruct((depth, rows, n_out), F32),
        compiler_params=_params("parallel", "parallel"),
        name="adaln_mod",
    )(c_all, w_mod, b_mod.reshape(depth, 1, n_out))


def _prep_kernel(win_ref, bdwf_ref, bdc_ref, bds_ref, o_ref):
    for blk in range(D_FOURIER // MXU_K):
        sl = slice(blk * MXU_K, (blk + 1) * MXU_K)
        wf = bdwf_ref[sl, sl]
        ma = _dot3(bdc_ref[sl, sl], wf)
        mv = -_dot3(bds_ref[sl, sl], wf)
        wi = win_ref[:, sl]
        o_ref[:, sl] = _dot3(wi, ma).astype(BF16)
        o_ref[:, D_FOURIER + blk * MXU_K:D_FOURIER + (blk + 1) * MXU_K] = _dot3(wi, mv).astype(BF16)
    o_ref[:, 2 * D_FOURIER:] = win_ref[:, D_FOURIER:].astype(BF16)


def _block_diag(w):
    h, d, e = w.shape[-3:]
    eye = jnp.eye(h, dtype=w.dtype)
    out = w[..., :, :, None, :] * eye[:, None, :, None]
    return out.reshape(w.shape[:-3] + (h * d, h * e))


def _channel_dft_tables():
    k = np.arange(HEAD_DIM)
    ang = 2.0 * np.pi * np.outer(k, k) / HEAD_DIM
    eye = np.eye(N_HEADS)
    return (jnp.asarray(np.kron(eye, np.cos(ang)), F32), jnp.asarray(np.kron(eye, np.sin(ang)), F32))


def _fold_in_weights(w_in, w_fourier):
    depth = w_in.shape[0]
    d_in = w_in.shape[2]
    n_out = d_in + D_FOURIER
    bdc, bds = _channel_dft_tables()
    bdwf = _block_diag(w_fourier)
    return pl.pallas_call(
        _prep_kernel,
        grid=(depth,),
        in_specs=[
            pl.BlockSpec((None, D_MODEL, d_in), lambda l: (l, 0, 0)),
            pl.BlockSpec((None, D_FOURIER, D_FOURIER), lambda l: (l, 0, 0)),
            _const_spec((D_FOURIER, D_FOURIER)),
            _const_spec((D_FOURIER, D_FOURIER)),
        ],
        out_specs=pl.BlockSpec((None, D_MODEL, n_out), lambda l: (l, 0, 0)),
        out_shape=jax.ShapeDtypeStruct((depth, D_MODEL, n_out), BF16),
        compiler_params=_params("parallel"),
        name="fold_in_weights",
    )(w_in, bdwf, bdc, bds)


def _rms_norm(x, g):
    ms = jnp.mean(x * x, axis=-1, keepdims=True)
    return x * lax.rsqrt(ms + EPS) * g


def _gelu_tanh(x):
    c = math.sqrt(2.0 / math.pi)
    return 0.5 * x * (1.0 + jnp.tanh(c * (x + 0.044715 * (x * x * x))))


def _mod_spec(layer, row0):
    return pl.BlockSpec((None, None, 1, N_MOD * D_MODEL), lambda b, i: (layer, row0 + b, 0, 0))


def _slab_shape(b, s):
    return jax.ShapeDtypeStruct((b, N_SLAB, s // ROW_TILE * TILE_PITCH, LANES), F32)


def _slab_spec():
    return pl.BlockSpec((None, N_SLAB, TILE_PITCH, LANES), lambda bi, i: (bi, 0, i, 0))


def _inproj_kernel(x_ref, mod_ref, g_ref, w_ref, ua_ref, uv_ref, ur_ref, gg_ref, *, n1):
    shift = mod_ref[:, 0:D_MODEL]
    scale = mod_ref[:, D_MODEL:2 * D_MODEL]
    h = _rms_norm(x_ref[...], g_ref[...] * (1.0 + scale)) + shift
    z = jnp.dot(h.astype(BF16), w_ref[...], preferred_element_type=F32)
    n_grp = n1 // SUBLANES
    for c in range(N_SLAB):
        for s2 in range(ROW_TILE // n1):
            for grp in range(n_grp):
                src = slice(s2 * n1 + grp * SUBLANES, s2 * n1 + (grp + 1) * SUBLANES)
                dst = slice(s2 * SUBLANES, (s2 + 1) * SUBLANES)
                ua_ref[c, grp, dst, :] = z[src, c * LANES:(c + 1) * LANES]
                uv_ref[c, grp, dst, :] = z[src, D_FOURIER + c * LANES:D_FOURIER + (c + 1) * LANES]
    r0 = 2 * D_FOURIER
    g0 = r0 + D_LRU
    pad = jnp.zeros((SEG_PITCH - SEG, LANES), F32)
    for c in range(N_SLAB):
        ur = z[:, r0 + c * LANES:r0 + (c + 1) * LANES]
        for j in range(SUBLANES):
            ur_ref[c, j * SEG_PITCH:j * SEG_PITCH + SEG, :] = ur[j * SEG:(j + 1) * SEG]
            ur_ref[c, j * SEG_PITCH + SEG:(j + 1) * SEG_PITCH, :] = pad
    gg_ref[...] = _gelu_tanh(z[:, g0:]).astype(BF16)


def _in_projection(x, mod, layer, row0, g_mix, w_fold):
    b, s, _ = x.shape
    n_out = w_fold.shape[-1]
    slab = _slab_shape(b, s)
    slab_spec = _slab_spec()
    n1 = s // DFT_N2
    n_grp = n1 // SUBLANES
    grp_rows = ROW_TILE // n1 * SUBLANES
    half = jax.ShapeDtypeStruct((b, N_SLAB, n_grp, DFT_N2 * SUBLANES, LANES), F32)
    half_spec = pl.BlockSpec((None, N_SLAB, n_grp, grp_rows, LANES), lambda bi, i: (bi, 0, 0, i, 0))
    return pl.pallas_call(
        functools.partial(_inproj_kernel, n1=n1),
        grid=(b, s // ROW_TILE),
        in_specs=[
            pl.BlockSpec((None, ROW_TILE, D_MODEL), lambda bi, i: (bi, i, 0)),
            _mod_spec(layer, row0),
            _const_spec((1, D_MODEL)),
            _layer_spec(layer, (D_MODEL, n_out)),
        ],
        out_specs=[half_spec, half_spec, slab_spec,
                   pl.BlockSpec((None, ROW_TILE, D_LRU), lambda bi, i: (bi, i, 0))],
        out_shape=[half, half, slab, jax.ShapeDtypeStruct((b, s, D_LRU), BF16)],
        compiler_params=_params("parallel", "parallel"),
        name="inproj",
    )(x, mod, g_mix.reshape(1, D_MODEL), w_fold)


def _dft_tables(seq):
    n2 = DFT_N2
    n1 = seq // n2
    k2 = np.arange(n2)
    ang = 2.0 * np.pi * np.outer(k2, k2) / n2
    c2 = np.cos(ang) / math.sqrt(n2)
    s2 = np.sin(ang) / math.sqrt(n2)
    fa = np.concatenate([c2, -s2], axis=0)
    fv = np.concatenate([s2, c2], axis=0)
    g = DFT_GROUP
    n_groups = n2 // g
    rows = n1 * g
    k1 = np.arange(n1)
    s1 = np.arange(n1)
    kp = np.arange(g)
    scale = 1.0 / math.sqrt(n1 * HEAD_DIM)
    lc = np.zeros((n_groups, n1, g, n1, g), np.float64)
    ls = np.zeros((n_groups, n1, g, n1, g), np.float64)
    for grp in range(n_groups):
        k = n2 * k1[:, None] + (g * grp + kp)[None, :]
        ang_b = 2.0 * np.pi * (k[:, :, None] * s1[None, None, :] % seq) / seq
        for q in range(g):
            lc[grp, :, q, :, q] = np.cos(ang_b[:, q, :]) * scale
            ls[grp, :, q, :, q] = np.sin(ang_b[:, q, :]) * scale
    lc = lc.reshape(n_groups, rows, rows)
    ls = ls.reshape(n_groups, rows, rows)
    as_bf16 = lambda a: jnp.asarray(a.astype(np.float32)).astype(BF16)
    return as_bf16(fa), as_bf16(fv), as_bf16(lc), as_bf16(ls)


def _dft_kernel(a_ref, v_ref, fa_ref, fv_ref, lc_ref, ls_ref, y_ref, tre_scr, tim_scr, *, n1, groups_per_step):
    k = pl.program_id(1)
    n_grp = n1 // SUBLANES

    @pl.when(k < n_grp)
    def _():
        for q in range(SUBLANES):
            rows = pl.ds(q, DFT_N2, stride=SUBLANES)
            a = jnp.concatenate([a_ref[c, rows, :] for c in range(N_SLAB)], axis=1).astype(BF16)
            v = jnp.concatenate([v_ref[c, rows, :] for c in range(N_SLAB)], axis=1).astype(BF16)
            o = (jnp.dot(fa_ref[...], a, preferred_element_type=F32)
                 + jnp.dot(fv_ref[...], v, preferred_element_type=F32))
            tre_scr[k * SUBLANES + q] = o[0:DFT_N2].astype(BF16)
            tim_scr[k * SUBLANES + q] = o[DFT_N2:].astype(BF16)

    @pl.when(k >= n_grp)
    def _():
        rows = n1 * DFT_GROUP
        for j in range(groups_per_step):
            first = pl.multiple_of(((k - n_grp) * groups_per_step + j) * DFT_GROUP, DFT_GROUP)
            tr = tre_scr[:, pl.ds(first, DFT_GROUP), :].reshape(rows, D_FOURIER)
            ti = tim_scr[:, pl.ds(first, DFT_GROUP), :].reshape(rows, D_FOURIER)
            o = (jnp.dot(lc_ref[j], tr, preferred_element_type=F32)
                 + jnp.dot(ls_ref[j], ti, preferred_element_type=F32))
            y_ref[:, j * DFT_GROUP:(j + 1) * DFT_GROUP, :] = o.astype(BF16).reshape(n1, DFT_GROUP, D_FOURIER)


def _fourier_mix(ua, uv, s):
    b = ua.shape[0]
    n2 = DFT_N2
    n1 = s // n2
    n_grp = n1 // SUBLANES
    rows = n1 * DFT_GROUP
    groups_per_step = DFT_ROWS_PER_STEP // rows
    n_out = n2 // (DFT_GROUP * groups_per_step)
    fa, fv, lc, ls = _dft_tables(s)
    in_spec = pl.BlockSpec((None, N_SLAB, None, n2 * SUBLANES, LANES),
                           lambda bi, k: (bi, 0, jnp.minimum(k, n_grp - 1), 0, 0))
    l_spec = pl.BlockSpec((groups_per_step, rows, rows), lambda bi, k: (jnp.maximum(k - n_grp, 0), 0, 0))
    t_scr = pltpu.VMEM((n1, n2, D_FOURIER), BF16)
    y = pl.pallas_call(
        functools.partial(_dft_kernel, n1=n1, groups_per_step=groups_per_step),
        grid=(b, n_grp + n_out),
        in_specs=[in_spec, in_spec, _const_spec((2 * n2, n2)), _const_spec((2 * n2, n2)), l_spec, l_spec],
        out_specs=pl.BlockSpec((None, n1, DFT_GROUP * groups_per_step, D_FOURIER),
                               lambda bi, k: (bi, 0, jnp.maximum(k - n_grp, 0), 0)),
        out_shape=jax.ShapeDtypeStruct((b, n1, n2, D_FOURIER), BF16),
        scratch_shapes=[t_scr, t_scr],
        compiler_params=_params("parallel", "arbitrary"),
        name="seq_dft",
    )(ua, uv, fa, fv, lc, ls)
    return y.reshape(b, s, D_FOURIER)


def _segment_rows(ref, c, t):
    return ref[c, pl.ds(t, SUBLANES, stride=SEG_PITCH), :]


def _lru_kernel(ur_ref, prev_ref, next_ref, cw_ref, cb_ref, wg_ref, bg_ref, lam_ref,
                yl_ref, pf_ref, pb_ref, ah_ref, uc_scr, pre_scr, hf_scr):
    i = pl.program_id(1)
    n_tiles = pl.num_programs(1)
    sub = lax.broadcasted_iota(jnp.int32, (SUBLANES, LANES), 0)
    slab_lanes = [slice(c * LANES, (c + 1) * LANES) for c in range(N_SLAB)]

    pad = jnp.zeros((SEG_PITCH - SEG, LANES), F32)
    for c in range(N_SLAB):
        for j in range(SUBLANES):
            pad_rows = slice(j * SEG_PITCH + SEG, (j + 1) * SEG_PITCH)
            yl_ref[c, pad_rows, :] = pad
            pf_ref[c, pad_rows, :] = pad
            pb_ref[c, pad_rows, :] = pad

    for c in range(N_SLAB):
        lanes = slab_lanes[c]
        prev_rows = [jnp.where(i > 0, prev_ref[c, SUBLANES - 2 + q:SUBLANES - 1 + q, :], 0.0) for q in range(2)]
        next_row = jnp.where(i < n_tiles - 1, next_ref[c, 0:1, :], 0.0)
        first = _segment_rows(ur_ref, c, 0)
        window = [jnp.where(sub == 0, prev_rows[q], pltpu.roll(_segment_rows(ur_ref, c, SEG - 2 + q), 1, 0))
                  for q in range(2)] + [first]
        cw = [jnp.broadcast_to(0.5 * cw_ref[k:k + 1, lanes], (SUBLANES, LANES)) for k in range(CONV_WIDTH)]
        cbias = jnp.broadcast_to(0.5 * cb_ref[:, lanes], (SUBLANES, LANES))
        for t in range(SEG):
            if t + 1 < SEG:
                nxt = _segment_rows(ur_ref, c, t + 1)
            else:
                nxt = jnp.where(sub == SUBLANES - 1, next_row, pltpu.roll(first, SUBLANES - 1, 0))
            window.append(nxt)
            uc = cbias + window[0] * cw[0] + window[1] * cw[1] + window[2] * cw[2] + window[3] * cw[3]
            uc_scr[t * SUBLANES:(t + 1) * SUBLANES, lanes] = uc
            window.pop(0)

    for hh in range(D_LRU // MXU_K):
        cols = slice(hh * GATE_COLS, (hh + 1) * GATE_COLS)
        lhs = uc_scr[:, hh * MXU_K:(hh + 1) * MXU_K].astype(BF16)
        pre_scr[:, cols] = jnp.dot(lhs, wg_ref[hh], preferred_element_type=F32) + bg_ref[:, cols]

    neg_lam = -lam_ref[...]
    softplus = jnp.maximum(neg_lam, 0.0) + jnp.log1p(jnp.exp(-jnp.abs(neg_lam)))
    k_all = (-0.5 * LRU_C * LOG2E) * softplus
    k_rows = [[jnp.broadcast_to(k_all[d:d + 1, slab_lanes[c]], (SUBLANES, LANES)) for c in range(N_SLAB)]
              for d in range(2)]

    def scan_inputs(rows, c, d):
        col = (c * LANES // MXU_K) * GATE_COLS + d * 2 * MXU_K + (c * LANES) % MXU_K
        th_r = jnp.tanh(pre_scr[rows, col:col + LANES])
        th_i = jnp.tanh(pre_scr[rows, col + MXU_K:col + MXU_K + LANES])
        k = k_rows[d][c]
        a = jnp.exp2(k * th_r + k)
        m = 1.0 - a * a
        mult = jnp.where(m > 0.0, m * lax.rsqrt(m), 0.0)
        return a, mult * ((th_i + 1.0) * uc_scr[rows, slab_lanes[c]])

    hs = [jnp.zeros((SUBLANES, LANES), F32) for _ in range(N_SLAB)]
    ps = [jnp.ones((SUBLANES, LANES), F32) for _ in range(N_SLAB)]
    for t in range(SEG):
        rows = slice(t * SUBLANES, (t + 1) * SUBLANES)
        for c in range(N_SLAB):
            a, b = scan_inputs(rows, c, 0)
            hs[c] = a * hs[c] + b
            ps[c] = a * ps[c]
            hf_scr[rows, slab_lanes[c]] = hs[c]
            pf_ref[c, pl.ds(t, SUBLANES, stride=SEG_PITCH), :] = ps[c]
    for c in range(N_SLAB):
        ah_ref[0, :, slab_lanes[c]] = ps[c]
        ah_ref[1, :, slab_lanes[c]] = hs[c]

    hs = [jnp.zeros((SUBLANES, LANES), F32) for _ in range(N_SLAB)]
    ps = [jnp.ones((SUBLANES, LANES), F32) for _ in range(N_SLAB)]
    for t in reversed(range(SEG)):
        rows = slice(t * SUBLANES, (t + 1) * SUBLANES)
        for c in range(N_SLAB):
            a, b = scan_inputs(rows, c, 1)
            hs[c] = a * hs[c] + b
            ps[c] = a * ps[c]
            yl_ref[c, pl.ds(t, SUBLANES, stride=SEG_PITCH), :] = hf_scr[rows, slab_lanes[c]] + hs[c]
            pb_ref[c, pl.ds(t, SUBLANES, stride=SEG_PITCH), :] = ps[c]
    for c in range(N_SLAB):
        ah_ref[2, :, slab_lanes[c]] = ps[c]
        ah_ref[3, :, slab_lanes[c]] = hs[c]


def _lru_local(ur, layer, conv_w, conv_b, wg, bg, lam):
    b, _, rows, _ = ur.shape
    n_tiles = rows // TILE_PITCH
    blocks_per_tile = TILE_PITCH // SUBLANES
    last_block = rows // SUBLANES - 1
    s = n_tiles * ROW_TILE
    slab = _slab_shape(b, s)
    slab_spec = _slab_spec()
    prev_spec = pl.BlockSpec((None, N_SLAB, SUBLANES, LANES),
                             lambda bi, i: (bi, 0, jnp.maximum(i * blocks_per_tile - 2, 0), 0))
    next_spec = pl.BlockSpec((None, N_SLAB, SUBLANES, LANES),
                             lambda bi, i: (bi, 0, jnp.minimum((i + 1) * blocks_per_tile, last_block), 0))
    n_seg = s // SEG
    n_half = D_LRU // MXU_K
    tile_f32 = pltpu.VMEM((ROW_TILE, D_LRU), F32)
    return pl.pallas_call(
        _lru_kernel,
        grid=(b, n_tiles),
        in_specs=[
            slab_spec, prev_spec, next_spec,
            _const_spec((CONV_WIDTH, D_LRU)),
            _const_spec((1, D_LRU)),
            _layer_spec(layer, (n_half, MXU_K, GATE_COLS)),
            _const_spec((1, n_half * GATE_COLS)),
            _const_spec((2, D_LRU)),
        ],
        out_specs=[slab_spec, slab_spec, slab_spec,
                   pl.BlockSpec((None, 4, SUBLANES, D_LRU), lambda bi, i: (bi, 0, i, 0))],
        out_shape=[slab, slab, slab, jax.ShapeDtypeStruct((b, 4, n_seg, D_LRU), F32)],
        scratch_shapes=[tile_f32, pltpu.VMEM((ROW_TILE, n_half * GATE_COLS), F32), tile_f32],
        compiler_params=_params("parallel", "parallel"),
        name="lru_local",
    )(ur, ur, ur, conv_w, conv_b.reshape(1, D_LRU), wg, bg, lam)


def _carry_kernel(ah_ref, cf_ref, cb_ref, *, n_seg):
    def step(n, carry):
        cf, cb = carry
        sf = n
        sb = n_seg - 1 - n
        cf_ref[pl.ds(sf, 1), :] = cf
        cb_ref[pl.ds(sb, 1), :] = cb
        return (ah_ref[1, pl.ds(sf, 1), :] + ah_ref[0, pl.ds(sf, 1), :] * cf,
                ah_ref[3, pl.ds(sb, 1), :] + ah_ref[2, pl.ds(sb, 1), :] * cb)

    zero = jnp.zeros((1, D_LRU), F32)
    lax.fori_loop(0, n_seg, step, (zero, zero), unroll=CARRY_UNROLL)


def _segment_carries(ah):
    b, _, n_seg, _ = ah.shape
    c_shape = jax.ShapeDtypeStruct((b, n_seg, D_LRU), F32)
    c_spec = pl.BlockSpec((None, n_seg, D_LRU), lambda bi: (bi, 0, 0))
    return pl.pallas_call(
        functools.partial(_carry_kernel, n_seg=n_seg),
        grid=(b,),
        in_specs=[pl.BlockSpec((None, 4, n_seg, D_LRU), lambda bi: (bi, 0, 0, 0))],
        out_specs=[c_spec, c_spec],
        out_shape=[c_shape, c_shape],
        compiler_params=_params("parallel"),
        name="lru_carry",
    )(ah)


def _ffn_kernel(x_ref, yf_ref, gg_ref, yl_ref, pf_ref, pb_ref, cf_ref, cb_ref, mod_ref, g_ref,
                wo_ref, w1_ref, w2_ref, gfin_ref, o_ref, yr_scr, *, final_norm):
    for c in range(N_SLAB):
        lanes = slice(c * LANES, (c + 1) * LANES)
        for j in range(ROW_TILE // SEG):
            rows = slice(j * SEG_PITCH, j * SEG_PITCH + SEG)
            hsum = (yl_ref[c, rows, :] + pf_ref[c, rows, :] * cf_ref[j:j + 1, lanes]
                    + pb_ref[c, rows, :] * cb_ref[j:j + 1, lanes])
            gate = gg_ref[j * SEG:(j + 1) * SEG, lanes].astype(F32)
            yr_scr[j * SEG:(j + 1) * SEG, lanes] = (hsum * gate).astype(BF16)
    gate1 = mod_ref[:, 2 * D_MODEL:3 * D_MODEL]
    shift2 = mod_ref[:, 3 * D_MODEL:4 * D_MODEL]
    scale2 = mod_ref[:, 4 * D_MODEL:5 * D_MODEL]
    gate2 = mod_ref[:, 5 * D_MODEL:6 * D_MODEL]
    mix = (jnp.dot(yf_ref[...], wo_ref[0:D_FOURIER, :], preferred_element_type=F32)
           + jnp.dot(yr_scr[...], wo_ref[D_FOURIER:, :], preferred_element_type=F32))
    x1 = x_ref[...] + gate1 * mix
    h = (_rms_norm(x1, g_ref[...] * (1.0 + scale2)) + shift2).astype(BF16)
    acc = jnp.zeros((ROW_TILE, D_MODEL), F32)
    for k in range(D_FF // FF_CHUNK):
        cols = slice(k * FF_CHUNK, (k + 1) * FF_CHUNK)
        t = jnp.maximum(jnp.dot(h, w1_ref[:, cols], preferred_element_type=F32), 0.0)
        acc = acc + jnp.dot((t * t).astype(BF16), w2_ref[cols, :], preferred_element_type=F32)
    x2 = x1 + gate2 * acc
    if final_norm:
        x2 = _rms_norm(x2, gfin_ref[...])
    o_ref[...] = x2


def _out_and_mlp(x, yf, gg, yl, pf, pb, cf, cb, mod, layer, row0, g_mlp, w_out, w_ff1, w_ff2, g_final,
                 final_norm):
    b, s, _ = x.shape
    segs = ROW_TILE // SEG
    x_spec = pl.BlockSpec((None, ROW_TILE, D_MODEL), lambda bi, i: (bi, i, 0))
    slab_spec = _slab_spec()
    c_spec = pl.BlockSpec((None, segs, D_LRU), lambda bi, i: (bi, i, 0))
    return pl.pallas_call(
        functools.partial(_ffn_kernel, final_norm=final_norm),
        grid=(b, s // ROW_TILE),
        in_specs=[
            x_spec,
            pl.BlockSpec((None, ROW_TILE, D_FOURIER), lambda bi, i: (bi, i, 0)),
            pl.BlockSpec((None, ROW_TILE, D_LRU), lambda bi, i: (bi, i, 0)),
            slab_spec, slab_spec, slab_spec, c_spec, c_spec,
            _mod_spec(layer, row0),
            _const_spec((1, D_MODEL)),
            _layer_spec(layer, (D_MODEL, D_MODEL)),
            _layer_spec(layer, (D_MODEL, D_FF)),
            _layer_spec(layer, (D_FF, D_MODEL)),
            _const_spec((1, D_MODEL)),
        ],
        out_specs=x_spec,
        out_shape=jax.ShapeDtypeStruct(x.shape, F32),
        scratch_shapes=[pltpu.VMEM((ROW_TILE, D_LRU), BF16)],
        compiler_params=_params("parallel", "parallel"),
        name="outproj_mlp",
    )(x, yf, gg, yl, pf, pb, cf, cb, mod, g_mlp.reshape(1, D_MODEL), w_out, w_ff1, w_ff2,
      g_final.reshape(1, D_MODEL))


def _gate_weights(w_rgate, b_rgate, w_igate, b_igate):
    depth = w_rgate.shape[0]
    n_half = D_LRU // MXU_K
    heads_per_half = N_HEADS // n_half
    w_halves, b_halves = [], []
    for hh in range(n_half):
        hs = slice(hh * heads_per_half, (hh + 1) * heads_per_half)
        w_cols, b_cols = [], []
        for d in range(2):
            for w, bias in ((w_rgate, b_rgate), (w_igate, b_igate)):
                w_cols.append(_block_diag(w[:, d, hs]))
                b_cols.append(bias[:, d, hs].reshape(depth, MXU_K))
        w_halves.append(jnp.concatenate(w_cols, axis=-1))
        b_halves.append(jnp.concatenate(b_cols, axis=-1))
    wg = jnp.stack(w_halves, axis=1).astype(BF16)
    bg = (0.5 * jnp.concatenate(b_halves, axis=-1)).reshape(depth, 1, n_half * GATE_COLS)
    return wg, bg


def _trunk(x, mod, row0, layers, g_final):
    depth = len(layers)
    for l, p in enumerate(layers):
        ua, uv, ur, gg = _in_projection(x, mod, l, row0, p["g_mix"], p["w_fold"])
        yf = _fourier_mix(ua, uv, x.shape[1])
        yl, pf, pb, ah = _lru_local(ur, l, p["conv_w"], p["conv_b"], p["wg"], p["bg"], p["lam"])
        cf, cb = _segment_carries(ah)
        x = _out_and_mlp(x, yf, gg, yl, pf, pb, cf, cb, mod, l, row0, p["g_mlp"], p["w_out"],
                         p["w_ff1"], p["w_ff2"], g_final, final_norm=(l == depth - 1))
    return x


def kernel(x_prompt, x_sample, c_prompt, c_sample, g_mix, g_mlp, w_mod, b_mod, w_in, w_fourier,
           conv_w, conv_b, w_rgate, b_rgate, w_igate, b_igate, lru_lambda, w_out, w_ff1, w_ff2,
           g_final):
    depth = w_in.shape[0]
    n_prompt = c_prompt.shape[0]
    n_cond = n_prompt + c_sample.shape[0]
    pad = -n_cond % SUBLANES
    c_all = jnp.concatenate([c_prompt, c_sample, jnp.zeros((pad, D_MODEL), F32)], axis=0)
    mod = _modulation(c_all, w_mod, b_mod)
    mod = mod.reshape(depth, n_cond + pad, 1, N_MOD * D_MODEL)

    w_fold = _fold_in_weights(w_in, w_fourier)
    wg, bg = _gate_weights(w_rgate, b_rgate, w_igate, b_igate)
    w_out_b = w_out.astype(BF16)
    w_ff1_b = w_ff1.astype(BF16)
    w_ff2_b = w_ff2.astype(BF16)
    layers = [dict(g_mix=g_mix[l], g_mlp=g_mlp[l], w_fold=w_fold, conv_w=conv_w[l], conv_b=conv_b[l],
                   wg=wg, bg=bg[l], lam=lru_lambda[l], w_out=w_out_b, w_ff1=w_ff1_b,
                   w_ff2=w_ff2_b) for l in range(depth)]
    y_prompt = _trunk(x_prompt, mod, 0, layers, g_final)
    y_sample = _trunk(x_sample, mod, n_prompt, layers, g_final)
    return (y_prompt, y_sample)
```

```python
import functools
import math

import numpy as np
import jax
import jax.numpy as jnp
from jax import lax
from jax.experimental import pallas as pl
from jax.experimental.pallas import tpu as pltpu

F32 = jnp.float32
BF16 = jnp.bfloat16

D_MODEL = 1024
HEAD_DIM = 64
D_FOURIER = 512
D_LRU = 512
N_HEADS = 8
D_FF = 4096
N_MOD = 6
EPS = 1e-6
LRU_C = 8.0
CONV_WIDTH = 4

LANES = 128
SUBLANES = 8
BF16_ROWS = 16
N_SLAB = D_LRU // LANES
ROW_TILE = 512
IN_TILE = 1024
SEG = ROW_TILE // SUBLANES
SEG_PITCH = SEG + SUBLANES
TILE_PITCH = SUBLANES * SEG_PITCH
LRU_TILES_PER_STEP = 2
LOG2E = 1.4426950408889634
MXU_K = 256
GATE_COLS = 4 * MXU_K
DFT_N2 = 256
DFT_GROUP = BF16_ROWS
DFT_ROWS_PER_STEP = 1024
CARRY_UNROLL = 4
FF_CHUNK = 1024
MOD_COLS = 1536
VMEM_LIMIT = 56 * 1024 * 1024


def _const_spec(shape):
    n = len(shape)
    return pl.BlockSpec(shape, lambda *_: (0,) * n, pipeline_mode=pl.Buffered(1))


def _layer_spec(layer, shape):
    n = len(shape)
    return pl.BlockSpec((None,) + tuple(shape), lambda *_: (layer,) + (0,) * n, pipeline_mode=pl.Buffered(1))


def _dot3(a, b):
    a_hi = a.astype(BF16)
    b_hi = b.astype(BF16)
    a_lo = (a - a_hi.astype(F32)).astype(BF16)
    b_lo = (b - b_hi.astype(F32)).astype(BF16)
    return (jnp.dot(a_hi, b_hi, preferred_element_type=F32) + jnp.dot(a_hi, b_lo, preferred_element_type=F32)
            + jnp.dot(a_lo, b_hi, preferred_element_type=F32))


def _params(*sem):
    return pltpu.CompilerParams(dimension_semantics=sem, vmem_limit_bytes=VMEM_LIMIT)


def _mod_kernel(c_ref, w_ref, b_ref, o_ref):
    c = c_ref[...]
    s = c * jax.nn.sigmoid(c)
    o_ref[...] = _dot3(s, w_ref[...]) + b_ref[...]


def _modulation(c_all, w_mod, b_mod):
    depth = w_mod.shape[0]
    n_out = N_MOD * D_MODEL
    rows = c_all.shape[0]
    return pl.pallas_call(
        _mod_kernel,
        grid=(depth, n_out // MOD_COLS),
        in_specs=[
            pl.BlockSpec((rows, D_MODEL), lambda l, j: (0, 0)),
            pl.BlockSpec((None, D_MODEL, MOD_COLS), lambda l, j: (l, 0, j)),
            pl.BlockSpec((None, 1, MOD_COLS), lambda l, j: (l, 0, j)),
        ],
        out_specs=pl.BlockSpec((None, rows, MOD_COLS), lambda l, j: (l, 0, j)),
        out_shape=jax.ShapeDtypeStruct((depth, rows, n_out), F32),
        compiler_params=_params("parallel", "parallel"),
        name="adaln_mod",
    )(c_all, w_mod, b_mod.reshape(depth, 1, n_out))


def _prep_kernel(win_ref, bdwf_ref, bdc_ref, bds_ref, o_ref):
    for blk in range(D_FOURIER // MXU_K):
        sl = slice(blk * MXU_K, (blk + 1) * MXU_K)
        wf = bdwf_ref[sl, sl]
        ma = _dot3(bdc_ref[sl, sl], wf)
        mv = -_dot3(bds_ref[sl, sl], wf)
        wi = win_ref[:, sl]
        o_ref[:, sl] = _dot3(wi, ma).astype(BF16)
        o_ref[:, D_FOURIER + blk * MXU_K:D_FOURIER + (blk + 1) * MXU_K] = _dot3(wi, mv).astype(BF16)
    o_ref[:, 2 * D_FOURIER:] = win_ref[:, D_FOURIER:].astype(BF16)


def _block_diag(w):
    h, d, e = w.shape[-3:]
    eye = jnp.eye(h, dtype=w.dtype)
    out = w[..., :, :, None, :] * eye[:, None, :, None]
    return out.reshape(w.shape[:-3] + (h * d, h * e))


def _channel_dft_tables():
    k = np.arange(HEAD_DIM)
    ang = 2.0 * np.pi * np.outer(k, k) / HEAD_DIM
    eye = np.eye(N_HEADS)
    return (jnp.asarray(np.kron(eye, np.cos(ang)), F32), jnp.asarray(np.kron(eye, np.sin(ang)), F32))


def _fold_in_weights(w_in, w_fourier):
    depth = w_in.shape[0]
    d_in = w_in.shape[2]
    n_out = d_in + D_FOURIER
    bdc, bds = _channel_dft_tables()
    bdwf = _block_diag(w_fourier)
    return pl.pallas_call(
        _prep_kernel,
        grid=(depth,),
        in_specs=[
            pl.BlockSpec((None, D_MODEL, d_in), lambda l: (l, 0, 0)),
            pl.BlockSpec((None, D_FOURIER, D_FOURIER), lambda l: (l, 0, 0)),
            _const_spec((D_FOURIER, D_FOURIER)),
            _const_spec((D_FOURIER, D_FOURIER)),
        ],
        out_specs=pl.BlockSpec((None, D_MODEL, n_out), lambda l: (l, 0, 0)),
        out_shape=jax.ShapeDtypeStruct((depth, D_MODEL, n_out), BF16),
        compiler_params=_params("parallel"),
        name="fold_in_weights",
    )(w_in, bdwf, bdc, bds)


def _rms_norm(x, g):
    ms = jnp.mean(x * x, axis=-1, keepdims=True)
    return x * lax.rsqrt(ms + EPS) * g


def _gelu_tanh(x):
    c = math.sqrt(2.0 / math.pi)
    return 0.5 * x * (1.0 + jnp.tanh(c * (x + 0.044715 * (x * x * x))))


def _mod_spec(layer, row0):
    return pl.BlockSpec((None, None, 1, N_MOD * D_MODEL), lambda b, i: (layer, row0 + b, 0, 0))


def _slab_shape(b, s):
    return jax.ShapeDtypeStruct((b, N_SLAB, s // ROW_TILE * TILE_PITCH, LANES), F32)


def _slab_spec():
    return pl.BlockSpec((None, N_SLAB, TILE_PITCH, LANES), lambda bi, i: (bi, 0, i, 0))


def _inproj_kernel(x_ref, mod_ref, g_ref, w_ref, ua_ref, uv_ref, ur_ref, gg_ref, *, n1):
    shift = mod_ref[:, 0:D_MODEL]
    scale = mod_ref[:, D_MODEL:2 * D_MODEL]
    h = _rms_norm(x_ref[...], g_ref[...] * (1.0 + scale)) + shift
    z = jnp.dot(h.astype(BF16), w_ref[...], preferred_element_type=F32)
    n_grp = n1 // SUBLANES
    for c in range(N_SLAB):
        for s2 in range(IN_TILE // n1):
            for grp in range(n_grp):
                src = slice(s2 * n1 + grp * SUBLANES, s2 * n1 + (grp + 1) * SUBLANES)
                dst = slice(s2 * SUBLANES, (s2 + 1) * SUBLANES)
                ua_ref[c, grp, dst, :] = z[src, c * LANES:(c + 1) * LANES]
                uv_ref[c, grp, dst, :] = z[src, D_FOURIER + c * LANES:D_FOURIER + (c + 1) * LANES]
    r0 = 2 * D_FOURIER
    g0 = r0 + D_LRU
    pad = jnp.zeros((SEG_PITCH - SEG, LANES), F32)
    for c in range(N_SLAB):
        ur = z[:, r0 + c * LANES:r0 + (c + 1) * LANES]
        for j in range(IN_TILE // SEG):
            ur_ref[c, j * SEG_PITCH:j * SEG_PITCH + SEG, :] = ur[j * SEG:(j + 1) * SEG]
            ur_ref[c, j * SEG_PITCH + SEG:(j + 1) * SEG_PITCH, :] = pad
    gg_ref[...] = _gelu_tanh(z[:, g0:]).astype(BF16)


def _in_projection(x, mod, layer, row0, g_mix, w_fold):
    b, s, _ = x.shape
    n_out = w_fold.shape[-1]
    slab = _slab_shape(b, s)
    slab_spec = pl.BlockSpec((None, N_SLAB, IN_TILE // SEG * SEG_PITCH, LANES), lambda bi, i: (bi, 0, i, 0))
    n1 = s // DFT_N2
    n_grp = n1 // SUBLANES
    grp_rows = IN_TILE // n1 * SUBLANES
    half = jax.ShapeDtypeStruct((b, N_SLAB, n_grp, DFT_N2 * SUBLANES, LANES), F32)
    half_spec = pl.BlockSpec((None, N_SLAB, n_grp, grp_rows, LANES), lambda bi, i: (bi, 0, 0, i, 0))
    return pl.pallas_call(
        functools.partial(_inproj_kernel, n1=n1),
        grid=(b, s // IN_TILE),
        in_specs=[
            pl.BlockSpec((None, IN_TILE, D_MODEL), lambda bi, i: (bi, i, 0)),
            _mod_spec(layer, row0),
            _const_spec((1, D_MODEL)),
            _layer_spec(layer, (D_MODEL, n_out)),
        ],
        out_specs=[half_spec, half_spec, slab_spec,
                   pl.BlockSpec((None, IN_TILE, D_LRU), lambda bi, i: (bi, i, 0))],
        out_shape=[half, half, slab, jax.ShapeDtypeStruct((b, s, D_LRU), BF16)],
        compiler_params=_params("parallel", "parallel"),
        name="inproj",
    )(x, mod, g_mix.reshape(1, D_MODEL), w_fold)


def _dft_tables(seq):
    n2 = DFT_N2
    n1 = seq // n2
    k2 = np.arange(n2)
    ang = 2.0 * np.pi * np.outer(k2, k2) / n2
    c2 = np.cos(ang) / math.sqrt(n2)
    s2 = np.sin(ang) / math.sqrt(n2)
    fa = np.concatenate([c2, -s2], axis=0)
    fv = np.concatenate([s2, c2], axis=0)
    g = DFT_GROUP
    n_groups = n2 // g
    rows = n1 * g
    k1 = np.arange(n1)
    s1 = np.arange(n1)
    kp = np.arange(g)
    scale = 1.0 / math.sqrt(n1 * HEAD_DIM)
    lc = np.zeros((n_groups, n1, g, n1, g), np.float64)
    ls = np.zeros((n_groups, n1, g, n1, g), np.float64)
    for grp in range(n_groups):
        k = n2 * k1[:, None] + (g * grp + kp)[None, :]
        ang_b = 2.0 * np.pi * (k[:, :, None] * s1[None, None, :] % seq) / seq
        for q in range(g):
            lc[grp, :, q, :, q] = np.cos(ang_b[:, q, :]) * scale
            ls[grp, :, q, :, q] = np.sin(ang_b[:, q, :]) * scale
    lc = lc.reshape(n_groups, rows, rows)
    ls = ls.reshape(n_groups, rows, rows)
    as_bf16 = lambda a: jnp.asarray(a.astype(np.float32)).astype(BF16)
    return as_bf16(fa), as_bf16(fv), as_bf16(lc), as_bf16(ls)


def _dft_kernel(a_ref, v_ref, fa_ref, fv_ref, lc_ref, ls_ref, y_ref, tre_scr, tim_scr, *, n1, groups_per_step):
    k = pl.program_id(1)
    n_grp = n1 // SUBLANES

    @pl.when(k < n_grp)
    def _():
        for q in range(SUBLANES):
            rows = pl.ds(q, DFT_N2, stride=SUBLANES)
            a = jnp.concatenate([a_ref[c, rows, :] for c in range(N_SLAB)], axis=1).astype(BF16)
            v = jnp.concatenate([v_ref[c, rows, :] for c in range(N_SLAB)], axis=1).astype(BF16)
            o = (jnp.dot(fa_ref[...], a, preferred_element_type=F32)
                 + jnp.dot(fv_ref[...], v, preferred_element_type=F32))
            tre_scr[k * SUBLANES + q] = o[0:DFT_N2].astype(BF16)
            tim_scr[k * SUBLANES + q] = o[DFT_N2:].astype(BF16)

    @pl.when(k >= n_grp)
    def _():
        rows = n1 * DFT_GROUP
        for j in range(groups_per_step):
            first = pl.multiple_of(((k - n_grp) * groups_per_step + j) * DFT_GROUP, DFT_GROUP)
            tr = tre_scr[:, pl.ds(first, DFT_GROUP), :].reshape(rows, D_FOURIER)
            ti = tim_scr[:, pl.ds(first, DFT_GROUP), :].reshape(rows, D_FOURIER)
            o = (jnp.dot(lc_ref[j], tr, preferred_element_type=F32)
                 + jnp.dot(ls_ref[j], ti, preferred_element_type=F32))
            y_ref[:, j * DFT_GROUP:(j + 1) * DFT_GROUP, :] = o.astype(BF16).reshape(n1, DFT_GROUP, D_FOURIER)


def _fourier_mix(ua, uv, s):
    b = ua.shape[0]
    n2 = DFT_N2
    n1 = s // n2
    n_grp = n1 // SUBLANES
    rows = n1 * DFT_GROUP
    groups_per_step = DFT_ROWS_PER_STEP // rows
    n_out = n2 // (DFT_GROUP * groups_per_step)
    fa, fv, lc, ls = _dft_tables(s)
    in_spec = pl.BlockSpec((None, N_SLAB, None, n2 * SUBLANES, LANES),
                           lambda bi, k: (bi, 0, jnp.minimum(k, n_grp - 1), 0, 0))
    l_spec = pl.BlockSpec((groups_per_step, rows, rows), lambda bi, k: (jnp.maximum(k - n_grp, 0), 0, 0))
    t_scr = pltpu.VMEM((n1, n2, D_FOURIER), BF16)
    y = pl.pallas_call(
        functools.partial(_dft_kernel, n1=n1, groups_per_step=groups_per_step),
        grid=(b, n_grp + n_out),
        in_specs=[in_spec, in_spec, _const_spec((2 * n2, n2)), _const_spec((2 * n2, n2)), l_spec, l_spec],
        out_specs=pl.BlockSpec((None, n1, DFT_GROUP * groups_per_step, D_FOURIER),
                               lambda bi, k: (bi, 0, jnp.maximum(k - n_grp, 0), 0)),
        out_shape=jax.ShapeDtypeStruct((b, n1, n2, D_FOURIER), BF16),
        scratch_shapes=[t_scr, t_scr],
        compiler_params=_params("parallel", "arbitrary"),
        name="seq_dft",
    )(ua, uv, fa, fv, lc, ls)
    return y.reshape(b, s, D_FOURIER)


def _lru_kernel(ur_ref, prev_ref, next_ref, cw_ref, cb_ref, wg_ref, bg_ref, lam_ref,
                yl_ref, pf_ref, pb_ref, ah_ref, uc_scr, pre_scr, hf_scr):
    i = pl.program_id(1)
    n_steps = pl.num_programs(1)
    sub = lax.broadcasted_iota(jnp.int32, (SUBLANES, LANES), 0)
    slab_lanes = [slice(c * LANES, (c + 1) * LANES) for c in range(N_SLAB)]

    cw = [[jnp.broadcast_to(0.5 * cw_ref[k:k + 1, slab_lanes[c]], (SUBLANES, LANES)) for k in range(CONV_WIDTH)]
          for c in range(N_SLAB)]
    cbias = [jnp.broadcast_to(0.5 * cb_ref[:, slab_lanes[c]], (SUBLANES, LANES)) for c in range(N_SLAB)]
    neg_lam = -lam_ref[...]
    softplus = jnp.maximum(neg_lam, 0.0) + jnp.log1p(jnp.exp(-jnp.abs(neg_lam)))
    k_all = (-0.5 * LRU_C * LOG2E) * softplus
    k_rows = [[jnp.broadcast_to(k_all[d:d + 1, slab_lanes[c]], (SUBLANES, LANES)) for c in range(N_SLAB)]
              for d in range(2)]
    pad = jnp.zeros((SEG_PITCH - SEG, LANES), F32)

    def scan_inputs(rows, c, d):
        col = (c * LANES // MXU_K) * GATE_COLS + d * 2 * MXU_K + (c * LANES) % MXU_K
        th_r = jnp.tanh(pre_scr[rows, col:col + LANES])
        th_i = jnp.tanh(pre_scr[rows, col + MXU_K:col + MXU_K + LANES])
        k = k_rows[d][c]
        a = jnp.exp2(k * th_r + k)
        m = 1.0 - a * a
        mult = jnp.where(m > 0.0, m * lax.rsqrt(m), 0.0)
        return a, mult * ((th_i + 1.0) * uc_scr[rows, slab_lanes[c]])

    for u in range(LRU_TILES_PER_STEP):
        base = u * TILE_PITCH
        last_seg_end = base - SEG_PITCH + SEG

        def segment_rows(ref, c, t, base=base):
            return ref[c, pl.ds(base + t, SUBLANES, stride=SEG_PITCH), :]

        for c in range(N_SLAB):
            for j in range(SUBLANES):
                pad_rows = slice(base + j * SEG_PITCH + SEG, base + (j + 1) * SEG_PITCH)
                yl_ref[c, pad_rows, :] = pad
                pf_ref[c, pad_rows, :] = pad
                pb_ref[c, pad_rows, :] = pad

        for c in range(N_SLAB):
            lanes = slab_lanes[c]
            if u == 0:
                prev_rows = [jnp.where(i > 0, prev_ref[c, SUBLANES - 2 + q:SUBLANES - 1 + q, :], 0.0) for q in range(2)]
            else:
                prev_rows = [ur_ref[c, last_seg_end - 2 + q:last_seg_end - 1 + q, :] for q in range(2)]
            if u == LRU_TILES_PER_STEP - 1:
                next_row = jnp.where(i < n_steps - 1, next_ref[c, 0:1, :], 0.0)
            else:
                next_row = ur_ref[c, base + TILE_PITCH:base + TILE_PITCH + 1, :]
            first = segment_rows(ur_ref, c, 0)
            window = [jnp.where(sub == 0, prev_rows[q], pltpu.roll(segment_rows(ur_ref, c, SEG - 2 + q), 1, 0))
                      for q in range(2)] + [first]
            for t in range(SEG):
                if t + 1 < SEG:
                    nxt = segment_rows(ur_ref, c, t + 1)
                else:
                    nxt = jnp.where(sub == SUBLANES - 1, next_row, pltpu.roll(first, SUBLANES - 1, 0))
                window.append(nxt)
                uc = (cbias[c] + window[0] * cw[c][0] + window[1] * cw[c][1] + window[2] * cw[c][2]
                      + window[3] * cw[c][3])
                uc_scr[t * SUBLANES:(t + 1) * SUBLANES, lanes] = uc
                window.pop(0)

        for hh in range(D_LRU // MXU_K):
            cols = slice(hh * GATE_COLS, (hh + 1) * GATE_COLS)
            lhs = uc_scr[:, hh * MXU_K:(hh + 1) * MXU_K].astype(BF16)
            pre_scr[:, cols] = jnp.dot(lhs, wg_ref[hh], preferred_element_type=F32) + bg_ref[:, cols]

        segs = slice(u * SUBLANES, (u + 1) * SUBLANES)
        hs = [jnp.zeros((SUBLANES, LANES), F32) for _ in range(N_SLAB)]
        ps = [jnp.ones((SUBLANES, LANES), F32) for _ in range(N_SLAB)]
        for t in range(SEG):
            rows = slice(t * SUBLANES, (t + 1) * SUBLANES)
            for c in range(N_SLAB):
                a, b = scan_inputs(rows, c, 0)
                hs[c] = a * hs[c] + b
                ps[c] = a * ps[c]
                hf_scr[rows, slab_lanes[c]] = hs[c]
                pf_ref[c, pl.ds(base + t, SUBLANES, stride=SEG_PITCH), :] = ps[c]
        for c in range(N_SLAB):
            ah_ref[0, segs, slab_lanes[c]] = ps[c]
            ah_ref[1, segs, slab_lanes[c]] = hs[c]

        hs = [jnp.zeros((SUBLANES, LANES), F32) for _ in range(N_SLAB)]
        ps = [jnp.ones((SUBLANES, LANES), F32) for _ in range(N_SLAB)]
        for t in reversed(range(SEG)):
            rows = slice(t * SUBLANES, (t + 1) * SUBLANES)
            for c in range(N_SLAB):
                a, b = scan_inputs(rows, c, 1)
                hs[c] = a * hs[c] + b
                ps[c] = a * ps[c]
                yl_ref[c, pl.ds(base + t, SUBLANES, stride=SEG_PITCH), :] = hf_scr[rows, slab_lanes[c]] + hs[c]
                pb_ref[c, pl.ds(base + t, SUBLANES, stride=SEG_PITCH), :] = ps[c]
        for c in range(N_SLAB):
            ah_ref[2, segs, slab_lanes[c]] = ps[c]
            ah_ref[3, segs, slab_lanes[c]] = hs[c]


def _lru_local(ur, layer, conv_w, conv_b, wg, bg, lam):
    b, _, rows, _ = ur.shape
    step_rows = LRU_TILES_PER_STEP * TILE_PITCH
    n_steps = rows // step_rows
    blocks_per_step = step_rows // SUBLANES
    last_block = rows // SUBLANES - 1
    s = rows // TILE_PITCH * ROW_TILE
    slab = _slab_shape(b, s)
    slab_spec = pl.BlockSpec((None, N_SLAB, step_rows, LANES), lambda bi, i: (bi, 0, i, 0))
    prev_spec = pl.BlockSpec((None, N_SLAB, SUBLANES, LANES),
                             lambda bi, i: (bi, 0, jnp.maximum(i * blocks_per_step - 2, 0), 0))
    next_spec = pl.BlockSpec((None, N_SLAB, SUBLANES, LANES),
                             lambda bi, i: (bi, 0, jnp.minimum((i + 1) * blocks_per_step, last_block), 0))
    n_seg = s // SEG
    n_half = D_LRU // MXU_K
    tile_f32 = pltpu.VMEM((ROW_TILE, D_LRU), F32)
    return pl.pallas_call(
        _lru_kernel,
        grid=(b, n_steps),
        in_specs=[
            slab_spec, prev_spec, next_spec,
            _const_spec((CONV_WIDTH, D_LRU)),
            _const_spec((1, D_LRU)),
            _layer_spec(layer, (n_half, MXU_K, GATE_COLS)),
            _const_spec((1, n_half * GATE_COLS)),
            _const_spec((2, D_LRU)),
        ],
        out_specs=[slab_spec, slab_spec, slab_spec,
                   pl.BlockSpec((None, 4, LRU_TILES_PER_STEP * SUBLANES, D_LRU), lambda bi, i: (bi, 0, i, 0))],
        out_shape=[slab, slab, slab, jax.ShapeDtypeStruct((b, 4, n_seg, D_LRU), F32)],
        scratch_shapes=[tile_f32, pltpu.VMEM((ROW_TILE, n_half * GATE_COLS), F32), tile_f32],
        compiler_params=_params("parallel", "parallel"),
        name="lru_local",
    )(ur, ur, ur, conv_w, conv_b.reshape(1, D_LRU), wg, bg, lam)


def _carry_kernel(ah_ref, cf_ref, cb_ref, *, n_seg):
    def step(n, carry):
        cf, cb = carry
        sf = n
        sb = n_seg - 1 - n
        cf_ref[pl.ds(sf, 1), :] = cf
        cb_ref[pl.ds(sb, 1), :] = cb
        return (ah_ref[1, pl.ds(sf, 1), :] + ah_ref[0, pl.ds(sf, 1), :] * cf,
                ah_ref[3, pl.ds(sb, 1), :] + ah_ref[2, pl.ds(sb, 1), :] * cb)

    zero = jnp.zeros((1, D_LRU), F32)
    lax.fori_loop(0, n_seg, step, (zero, zero), unroll=CARRY_UNROLL)


def _segment_carries(ah):
    b, _, n_seg, _ = ah.shape
    c_shape = jax.ShapeDtypeStruct((b, n_seg, D_LRU), F32)
    c_spec = pl.BlockSpec((None, n_seg, D_LRU), lambda bi: (bi, 0, 0))
    return pl.pallas_call(
        functools.partial(_carry_kernel, n_seg=n_seg),
        grid=(b,),
        in_specs=[pl.BlockSpec((None, 4, n_seg, D_LRU), lambda bi: (bi, 0, 0, 0))],
        out_specs=[c_spec, c_spec],
        out_shape=[c_shape, c_shape],
        compiler_params=_params("parallel"),
        name="lru_carry",
    )(ah)


def _ffn_kernel(x_ref, yf_ref, gg_ref, yl_ref, pf_ref, pb_ref, cf_ref, cb_ref, mod_ref, g_ref,
                wo_ref, w1_ref, w2_ref, gfin_ref, o_ref, yr_scr, *, final_norm):
    for c in range(N_SLAB):
        lanes = slice(c * LANES, (c + 1) * LANES)
        for j in range(ROW_TILE // SEG):
            rows = slice(j * SEG_PITCH, j * SEG_PITCH + SEG)
            hsum = (yl_ref[c, rows, :] + pf_ref[c, rows, :] * cf_ref[j:j + 1, lanes]
                    + pb_ref[c, rows, :] * cb_ref[j:j + 1, lanes])
            gate = gg_ref[j * SEG:(j + 1) * SEG, lanes].astype(F32)
            yr_scr[j * SEG:(j + 1) * SEG, lanes] = (hsum * gate).astype(BF16)
    gate1 = mod_ref[:, 2 * D_MODEL:3 * D_MODEL]
    shift2 = mod_ref[:, 3 * D_MODEL:4 * D_MODEL]
    scale2 = mod_ref[:, 4 * D_MODEL:5 * D_MODEL]
    gate2 = mod_ref[:, 5 * D_MODEL:6 * D_MODEL]
    mix = (jnp.dot(yf_ref[...], wo_ref[0:D_FOURIER, :], preferred_element_type=F32)
           + jnp.dot(yr_scr[...], wo_ref[D_FOURIER:, :], preferred_element_type=F32))
    x1 = x_ref[...] + gate1 * mix
    h = (_rms_norm(x1, g_ref[...] * (1.0 + scale2)) + shift2).astype(BF16)
    acc = jnp.zeros((ROW_TILE, D_MODEL), F32)
    for k in range(D_FF // FF_CHUNK):
        cols = slice(k * FF_CHUNK, (k + 1) * FF_CHUNK)
        t = jnp.maximum(jnp.dot(h, w1_ref[:, cols], preferred_element_type=F32), 0.0)
        acc = acc + jnp.dot((t * t).astype(BF16), w2_ref[cols, :], preferred_element_type=F32)
    x2 = x1 + gate2 * acc
    if final_norm:
        x2 = _rms_norm(x2, gfin_ref[...])
    o_ref[...] = x2


def _out_and_mlp(x, yf, gg, yl, pf, pb, cf, cb, mod, layer, row0, g_mlp, w_out, w_ff1, w_ff2, g_final,
                 final_norm):
    b, s, _ = x.shape
    segs = ROW_TILE // SEG
    x_spec = pl.BlockSpec((None, ROW_TILE, D_MODEL), lambda bi, i: (bi, i, 0))
    slab_spec = _slab_spec()
    c_spec = pl.BlockSpec((None, segs, D_LRU), lambda bi, i: (bi, i, 0))
    return pl.pallas_call(
        functools.partial(_ffn_kernel, final_norm=final_norm),
        grid=(b, s // ROW_TILE),
        in_specs=[
            x_spec,
            pl.BlockSpec((None, ROW_TILE, D_FOURIER), lambda bi, i: (bi, i, 0)),
            pl.BlockSpec((None, ROW_TILE, D_LRU), lambda bi, i: (bi, i, 0)),
            slab_spec, slab_spec, slab_spec, c_spec, c_spec,
            _mod_spec(layer, row0),
            _const_spec((1, D_MODEL)),
            _layer_spec(layer, (D_MODEL, D_MODEL)),
            _layer_spec(layer, (D_MODEL, D_FF)),
            _layer_spec(layer, (D_FF, D_MODEL)),
            _const_spec((1, D_MODEL)),
        ],
        out_specs=x_spec,
        out_shape=jax.ShapeDtypeStruct(x.shape, F32),
        scratch_shapes=[pltpu.VMEM((ROW_TILE, D_LRU), BF16)],
        compiler_params=_params("parallel", "parallel"),
        name="outproj_mlp",
    )(x, yf, gg, yl, pf, pb, cf, cb, mod, g_mlp.reshape(1, D_MODEL), w_out, w_ff1, w_ff2,
      g_final.reshape(1, D_MODEL))


def _gate_weights(w_rgate, b_rgate, w_igate, b_igate):
    depth = w_rgate.shape[0]
    n_half = D_LRU // MXU_K
    heads_per_half = N_HEADS // n_half
    w_halves, b_halves = [], []
    for hh in range(n_half):
        hs = slice(hh * heads_per_half, (hh + 1) * heads_per_half)
        w_cols, b_cols = [], []
        for d in range(2):
            for w, bias in ((w_rgate, b_rgate), (w_igate, b_igate)):
                w_cols.append(_block_diag(w[:, d, hs]))
                b_cols.append(bias[:, d, hs].reshape(depth, MXU_K))
        w_halves.append(jnp.concatenate(w_cols, axis=-1))
        b_halves.append(jnp.concatenate(b_cols, axis=-1))
    wg = jnp.stack(w_halves, axis=1).astype(BF16)
    bg = (0.5 * jnp.concatenate(b_halves, axis=-1)).reshape(depth, 1, n_half * GATE_COLS)
    return wg, bg


def _trunk(x, mod, row0, layers, g_final):
    depth = len(layers)
    for l, p in enumerate(layers):
        ua, uv, ur, gg = _in_projection(x, mod, l, row0, p["g_mix"], p["w_fold"])
        yf = _fourier_mix(ua, uv, x.shape[1])
        yl, pf, pb, ah = _lru_local(ur, l, p["conv_w"], p["conv_b"], p["wg"], p["bg"], p["lam"])
        cf, cb = _segment_carries(ah)
        x = _out_and_mlp(x, yf, gg, yl, pf, pb, cf, cb, mod, l, row0, p["g_mlp"], p["w_out"],
                         p["w_ff1"], p["w_ff2"], g_final, final_norm=(l == depth - 1))
    return x


def kernel(x_prompt, x_sample, c_prompt, c_sample, g_mix, g_mlp, w_mod, b_mod, w_in, w_fourier,
           conv_w, conv_b, w_rgate, b_rgate, w_igate, b_igate, lru_lambda, w_out, w_ff1, w_ff2,
           g_final):
    depth = w_in.shape[0]
    n_prompt = c_prompt.shape[0]
    n_cond = n_prompt + c_sample.shape[0]
    pad = -n_cond % SUBLANES
    c_all = jnp.concatenate([c_prompt, c_sample, jnp.zeros((pad, D_MODEL), F32)], axis=0)
    mod = _modulation(c_all, w_mod, b_mod)
    mod = mod.reshape(depth, n_cond + pad, 1, N_MOD * D_MODEL)

    w_fold = _fold_in_weights(w_in, w_fourier)
    wg, bg = _gate_weights(w_rgate, b_rgate, w_igate, b_igate)
    w_out_b = w_out.astype(BF16)
    w_ff1_b = w_ff1.astype(BF16)
    w_ff2_b = w_ff2.astype(BF16)
    layers = [dict(g_mix=g_mix[l], g_mlp=g_mlp[l], w_fold=w_fold, conv_w=conv_w[l], conv_b=conv_b[l],
                   wg=wg, bg=bg[l], lam=lru_lambda[l], w_out=w_out_b, w_ff1=w_ff1_b,
                   w_ff2=w_ff2_b) for l in range(depth)]
    y_prompt = _trunk(x_prompt, mod, 0, layers, g_final)
    y_sample = _trunk(x_sample, mod, n_prompt, layers, g_final)
    return (y_prompt, y_sample)
```

```python
import functools
import math

import numpy as np
import jax
import jax.numpy as jnp
from jax import lax
from jax.experimental import pallas as pl
from jax.experimental.pallas import tpu as pltpu

F32 = jnp.float32
BF16 = jnp.bfloat16

D_MODEL = 1024
HEAD_DIM = 64
D_FOURIER = 512
D_LRU = 512
N_HEADS = 8
D_FF = 4096
N_MOD = 6
EPS = 1e-6
LRU_C = 8.0
CONV_WIDTH = 4

LANES = 128
SUBLANES = 8
BF16_ROWS = 16
N_SLAB = D_LRU // LANES
ROW_TILE = 512
IN_TILE = 1024
SEG = ROW_TILE // SUBLANES
SEG_PITCH = SEG + SUBLANES
TILE_PITCH = SUBLANES * SEG_PITCH
LRU_TILES_PER_STEP = 4
N_STATE = 3
LOG2E = 1.4426950408889634
MXU_K = 256
GATE_COLS = 4 * MXU_K
DFT_N2 = 256
DFT_GROUP = BF16_ROWS
DFT_ROWS_PER_STEP = 1024
CARRY_UNROLL = 4
FF_CHUNK = 1024
MOD_COLS = 1536
VMEM_LIMIT = 56 * 1024 * 1024


def _const_spec(shape):
    n = len(shape)
    return pl.BlockSpec(shape, lambda *_: (0,) * n, pipeline_mode=pl.Buffered(1))


def _layer_spec(layer, shape):
    n = len(shape)
    return pl.BlockSpec((None,) + tuple(shape), lambda *_: (layer,) + (0,) * n, pipeline_mode=pl.Buffered(1))


def _dot3(a, b):
    a_hi = a.astype(BF16)
    b_hi = b.astype(BF16)
    a_lo = (a - a_hi.astype(F32)).astype(BF16)
    b_lo = (b - b_hi.astype(F32)).astype(BF16)
    return (jnp.dot(a_hi, b_hi, preferred_element_type=F32) + jnp.dot(a_hi, b_lo, preferred_element_type=F32)
            + jnp.dot(a_lo, b_hi, preferred_element_type=F32))


def _params(*sem):
    return pltpu.CompilerParams(dimension_semantics=sem, vmem_limit_bytes=VMEM_LIMIT)


def _mod_kernel(c_ref, w_ref, b_ref, o_ref):
    c = c_ref[...]
    s = c * jax.nn.sigmoid(c)
    o_ref[...] = _dot3(s, w_ref[...]) + b_ref[...]


def _modulation(c_all, w_mod, b_mod):
    depth = w_mod.shape[0]
    n_out = N_MOD * D_MODEL
    rows = c_all.shape[0]
    return pl.pallas_call(
        _mod_kernel,
        grid=(depth, n_out // MOD_COLS),
        in_specs=[
            pl.BlockSpec((rows, D_MODEL), lambda l, j: (0, 0)),
            pl.BlockSpec((None, D_MODEL, MOD_COLS), lambda l, j: (l, 0, j)),
            pl.BlockSpec((None, 1, MOD_COLS), lambda l, j: (l, 0, j)),
        ],
        out_specs=pl.BlockSpec((None, rows, MOD_COLS), lambda l, j: (l, 0, j)),
        out_shape=jax.ShapeDtypeStruct((depth, rows, n_out), F32),
        compiler_params=_params("parallel", "parallel"),
        name="adaln_mod",
    )(c_all, w_mod, b_mod.reshape(depth, 1, n_out))


def _prep_kernel(win_ref, bdwf_ref, bdc_ref, bds_ref, o_ref):
    for blk in range(D_FOURIER // MXU_K):
        sl = slice(blk * MXU_K, (blk + 1) * MXU_K)
        wf = bdwf_ref[sl, sl]
        ma = _dot3(bdc_ref[sl, sl], wf)
        mv = -_dot3(bds_ref[sl, sl], wf)
        wi = win_ref[:, sl]
        o_ref[:, sl] = _dot3(wi, ma).astype(BF16)
        o_ref[:, D_FOURIER + blk * MXU_K:D_FOURIER + (blk + 1) * MXU_K] = _dot3(wi, mv).astype(BF16)
    o_ref[:, 2 * D_FOURIER:] = win_ref[:, D_FOURIER:].astype(BF16)


def _block_diag(w):
    h, d, e = w.shape[-3:]
    eye = jnp.eye(h, dtype=w.dtype)
    out = w[..., :, :, None, :] * eye[:, None, :, None]
    return out.reshape(w.shape[:-3] + (h * d, h * e))


def _channel_dft_tables():
    k = np.arange(HEAD_DIM)
    ang = 2.0 * np.pi * np.outer(k, k) / HEAD_DIM
    eye = np.eye(N_HEADS)
    return (jnp.asarray(np.kron(eye, np.cos(ang)), F32), jnp.asarray(np.kron(eye, np.sin(ang)), F32))


def _fold_in_weights(w_in, w_fourier):
    depth = w_in.shape[0]
    d_in = w_in.shape[2]
    n_out = d_in + D_FOURIER
    bdc, bds = _channel_dft_tables()
    bdwf = _block_diag(w_fourier)
    return pl.pallas_call(
        _prep_kernel,
        grid=(depth,),
        in_specs=[
            pl.BlockSpec((None, D_MODEL, d_in), lambda l: (l, 0, 0)),
            pl.BlockSpec((None, D_FOURIER, D_FOURIER), lambda l: (l, 0, 0)),
            _const_spec((D_FOURIER, D_FOURIER)),
            _const_spec((D_FOURIER, D_FOURIER)),
        ],
        out_specs=pl.BlockSpec((None, D_MODEL, n_out), lambda l: (l, 0, 0)),
        out_shape=jax.ShapeDtypeStruct((depth, D_MODEL, n_out), BF16),
        compiler_params=_params("parallel"),
        name="fold_in_weights",
    )(w_in, bdwf, bdc, bds)


def _rms_norm(x, g):
    ms = jnp.mean(x * x, axis=-1, keepdims=True)
    return x * lax.rsqrt(ms + EPS) * g


def _gelu_tanh(x):
    c = math.sqrt(2.0 / math.pi)
    return 0.5 * x * (1.0 + jnp.tanh(c * (x + 0.044715 * (x * x * x))))


def _mod_spec(layer, row0):
    return pl.BlockSpec((None, None, 1, N_MOD * D_MODEL), lambda b, i: (layer, row0 + b, 0, 0))


def _slab_shape(b, s):
    return jax.ShapeDtypeStruct((b, N_SLAB, s // ROW_TILE * TILE_PITCH, LANES), F32)


def _inproj_kernel(x_ref, mod_ref, g_ref, w_ref, ua_ref, uv_ref, ur_ref, gg_ref, *, n1):
    shift = mod_ref[:, 0:D_MODEL]
    scale = mod_ref[:, D_MODEL:2 * D_MODEL]
    h = _rms_norm(x_ref[...], g_ref[...] * (1.0 + scale)) + shift
    z = jnp.dot(h.astype(BF16), w_ref[...], preferred_element_type=F32)
    n_grp = n1 // SUBLANES
    for c in range(N_SLAB):
        for s2 in range(IN_TILE // n1):
            for grp in range(n_grp):
                src = slice(s2 * n1 + grp * SUBLANES, s2 * n1 + (grp + 1) * SUBLANES)
                dst = slice(s2 * SUBLANES, (s2 + 1) * SUBLANES)
                ua_ref[c, grp, dst, :] = z[src, c * LANES:(c + 1) * LANES]
                uv_ref[c, grp, dst, :] = z[src, D_FOURIER + c * LANES:D_FOURIER + (c + 1) * LANES]
    r0 = 2 * D_FOURIER
    g0 = r0 + D_LRU
    pad = jnp.zeros((SEG_PITCH - SEG, LANES), F32)
    for c in range(N_SLAB):
        ur = z[:, r0 + c * LANES:r0 + (c + 1) * LANES]
        for j in range(IN_TILE // SEG):
            ur_ref[c, j * SEG_PITCH:j * SEG_PITCH + SEG, :] = ur[j * SEG:(j + 1) * SEG]
            ur_ref[c, j * SEG_PITCH + SEG:(j + 1) * SEG_PITCH, :] = pad
    gg_ref[...] = _gelu_tanh(z[:, g0:]).astype(BF16)


def _in_projection(x, mod, layer, row0, g_mix, w_fold):
    b, s, _ = x.shape
    n_out = w_fold.shape[-1]
    slab = _slab_shape(b, s)
    slab_spec = pl.BlockSpec((None, N_SLAB, IN_TILE // SEG * SEG_PITCH, LANES), lambda bi, i: (bi, 0, i, 0))
    n1 = s // DFT_N2
    n_grp = n1 // SUBLANES
    grp_rows = IN_TILE // n1 * SUBLANES
    half = jax.ShapeDtypeStruct((b, N_SLAB, n_grp, DFT_N2 * SUBLANES, LANES), F32)
    half_spec = pl.BlockSpec((None, N_SLAB, n_grp, grp_rows, LANES), lambda bi, i: (bi, 0, 0, i, 0))
    return pl.pallas_call(
        functools.partial(_inproj_kernel, n1=n1),
        grid=(b, s // IN_TILE),
        in_specs=[
            pl.BlockSpec((None, IN_TILE, D_MODEL), lambda bi, i: (bi, i, 0)),
            _mod_spec(layer, row0),
            _const_spec((1, D_MODEL)),
            _layer_spec(layer, (D_MODEL, n_out)),
        ],
        out_specs=[half_spec, half_spec, slab_spec,
                   pl.BlockSpec((None, IN_TILE, D_LRU), lambda bi, i: (bi, i, 0))],
        out_shape=[half, half, slab, jax.ShapeDtypeStruct((b, s, D_LRU), BF16)],
        compiler_params=_params("parallel", "parallel"),
        name="inproj",
    )(x, mod, g_mix.reshape(1, D_MODEL), w_fold)


def _dft_tables(seq):
    n2 = DFT_N2
    n1 = seq // n2
    k2 = np.arange(n2)
    ang = 2.0 * np.pi * np.outer(k2, k2) / n2
    c2 = np.cos(ang) / math.sqrt(n2)
    s2 = np.sin(ang) / math.sqrt(n2)
    fa = np.concatenate([c2, -s2], axis=0)
    fv = np.concatenate([s2, c2], axis=0)
    g = DFT_GROUP
    n_groups = n2 // g
    rows = n1 * g
    k1 = np.arange(n1)
    s1 = np.arange(n1)
    kp = np.arange(g)
    scale = 1.0 / math.sqrt(n1 * HEAD_DIM)
    lc = np.zeros((n_groups, n1, g, n1, g), np.float64)
    ls = np.zeros((n_groups, n1, g, n1, g), np.float64)
    for grp in range(n_groups):
        k = n2 * k1[:, None] + (g * grp + kp)[None, :]
        ang_b = 2.0 * np.pi * (k[:, :, None] * s1[None, None, :] % seq) / seq
        for q in range(g):
            lc[grp, :, q, :, q] = np.cos(ang_b[:, q, :]) * scale
            ls[grp, :, q, :, q] = np.sin(ang_b[:, q, :]) * scale
    lc = lc.reshape(n_groups, rows, rows)
    ls = ls.reshape(n_groups, rows, rows)
    as_bf16 = lambda a: jnp.asarray(a.astype(np.float32)).astype(BF16)
    return as_bf16(fa), as_bf16(fv), as_bf16(lc), as_bf16(ls)


def _dft_kernel(a_ref, v_ref, fa_ref, fv_ref, lc_ref, ls_ref, y_ref, tre_scr, tim_scr, *, n1, groups_per_step):
    k = pl.program_id(1)
    n_grp = n1 // SUBLANES

    @pl.when(k < n_grp)
    def _():
        for q in range(SUBLANES):
            rows = pl.ds(q, DFT_N2, stride=SUBLANES)
            a = jnp.concatenate([a_ref[c, rows, :] for c in range(N_SLAB)], axis=1).astype(BF16)
            v = jnp.concatenate([v_ref[c, rows, :] for c in range(N_SLAB)], axis=1).astype(BF16)
            o = (jnp.dot(fa_ref[...], a, preferred_element_type=F32)
                 + jnp.dot(fv_ref[...], v, preferred_element_type=F32))
            tre_scr[k * SUBLANES + q] = o[0:DFT_N2].astype(BF16)
            tim_scr[k * SUBLANES + q] = o[DFT_N2:].astype(BF16)

    @pl.when(k >= n_grp)
    def _():
        rows = n1 * DFT_GROUP
        for j in range(groups_per_step):
            first = pl.multiple_of(((k - n_grp) * groups_per_step + j) * DFT_GROUP, DFT_GROUP)
            tr = tre_scr[:, pl.ds(first, DFT_GROUP), :].reshape(rows, D_FOURIER)
            ti = tim_scr[:, pl.ds(first, DFT_GROUP), :].reshape(rows, D_FOURIER)
            o = (jnp.dot(lc_ref[j], tr, preferred_element_type=F32)
                 + jnp.dot(ls_ref[j], ti, preferred_element_type=F32))
            y_ref[:, j * DFT_GROUP:(j + 1) * DFT_GROUP, :] = o.astype(BF16).reshape(n1, DFT_GROUP, D_FOURIER)


def _fourier_mix(ua, uv, s):
    b = ua.shape[0]
    n2 = DFT_N2
    n1 = s // n2
    n_grp = n1 // SUBLANES
    rows = n1 * DFT_GROUP
    groups_per_step = DFT_ROWS_PER_STEP // rows
    n_out = n2 // (DFT_GROUP * groups_per_step)
    fa, fv, lc, ls = _dft_tables(s)
    in_spec = pl.BlockSpec((None, N_SLAB, None, n2 * SUBLANES, LANES),
                           lambda bi, k: (bi, 0, jnp.minimum(k, n_grp - 1), 0, 0))
    l_spec = pl.BlockSpec((groups_per_step, rows, rows), lambda bi, k: (jnp.maximum(k - n_grp, 0), 0, 0))
    t_scr = pltpu.VMEM((n1, n2, D_FOURIER), BF16)
    y = pl.pallas_call(
        functools.partial(_dft_kernel, n1=n1, groups_per_step=groups_per_step),
        grid=(b, n_grp + n_out),
        in_specs=[in_spec, in_spec, _const_spec((2 * n2, n2)), _const_spec((2 * n2, n2)), l_spec, l_spec],
        out_specs=pl.BlockSpec((None, n1, DFT_GROUP * groups_per_step, D_FOURIER),
                               lambda bi, k: (bi, 0, jnp.maximum(k - n_grp, 0), 0)),
        out_shape=jax.ShapeDtypeStruct((b, n1, n2, D_FOURIER), BF16),
        scratch_shapes=[t_scr, t_scr],
        compiler_params=_params("parallel", "arbitrary"),
        name="seq_dft",
    )(ua, uv, fa, fv, lc, ls)
    return y.reshape(b, s, D_FOURIER)


def _lru_kernel(ur_ref, prev_ref, next_ref, cw_ref, cb_ref, wg_ref, bg_ref, lam_ref,
                st_ref, ah_ref, uc_scr, pre_scr, hf_scr):
    i = pl.program_id(1)
    n_steps = pl.num_programs(1)
    sub = lax.broadcasted_iota(jnp.int32, (SUBLANES, LANES), 0)
    slab_lanes = [slice(c * LANES, (c + 1) * LANES) for c in range(N_SLAB)]

    cw = [[jnp.broadcast_to(0.5 * cw_ref[k:k + 1, slab_lanes[c]], (SUBLANES, LANES)) for k in range(CONV_WIDTH)]
          for c in range(N_SLAB)]
    cbias = [jnp.broadcast_to(0.5 * cb_ref[:, slab_lanes[c]], (SUBLANES, LANES)) for c in range(N_SLAB)]
    neg_lam = -lam_ref[...]
    softplus = jnp.maximum(neg_lam, 0.0) + jnp.log1p(jnp.exp(-jnp.abs(neg_lam)))
    k_all = (-0.5 * LRU_C * LOG2E) * softplus
    k_rows = [[jnp.broadcast_to(k_all[d:d + 1, slab_lanes[c]], (SUBLANES, LANES)) for c in range(N_SLAB)]
              for d in range(2)]
    pad = jnp.zeros((SEG_PITCH - SEG, LANES), F32)

    def scan_inputs(rows, c, d):
        col = (c * LANES // MXU_K) * GATE_COLS + d * 2 * MXU_K + (c * LANES) % MXU_K
        th_r = jnp.tanh(pre_scr[rows, col:col + LANES])
        th_i = jnp.tanh(pre_scr[rows, col + MXU_K:col + MXU_K + LANES])
        k = k_rows[d][c]
        a = jnp.exp2(k * th_r + k)
        m = 1.0 - a * a
        mult = jnp.where(m > 0.0, m * lax.rsqrt(m), 0.0)
        return a, mult * ((th_i + 1.0) * uc_scr[rows, slab_lanes[c]])

    for u in range(LRU_TILES_PER_STEP):
        base = u * TILE_PITCH
        last_seg_end = base - SEG_PITCH + SEG

        def segment_rows(ref, c, t, base=base):
            return ref[c, pl.ds(base + t, SUBLANES, stride=SEG_PITCH), :]

        for c in range(N_SLAB):
            for j in range(SUBLANES):
                pad_rows = slice(base + j * SEG_PITCH + SEG, base + (j + 1) * SEG_PITCH)
                for plane in range(N_STATE):
                    st_ref[plane, c, pad_rows, :] = pad

        for c in range(N_SLAB):
            lanes = slab_lanes[c]
            if u == 0:
                prev_rows = [jnp.where(i > 0, prev_ref[c, SUBLANES - 2 + q:SUBLANES - 1 + q, :], 0.0) for q in range(2)]
            else:
                prev_rows = [ur_ref[c, last_seg_end - 2 + q:last_seg_end - 1 + q, :] for q in range(2)]
            if u == LRU_TILES_PER_STEP - 1:
                next_row = jnp.where(i < n_steps - 1, next_ref[c, 0:1, :], 0.0)
            else:
                next_row = ur_ref[c, base + TILE_PITCH:base + TILE_PITCH + 1, :]
            first = segment_rows(ur_ref, c, 0)
            window = [jnp.where(sub == 0, prev_rows[q], pltpu.roll(segment_rows(ur_ref, c, SEG - 2 + q), 1, 0))
                      for q in range(2)] + [first]
            for t in range(SEG):
                if t + 1 < SEG:
                    nxt = segment_rows(ur_ref, c, t + 1)
                else:
                    nxt = jnp.where(sub == SUBLANES - 1, next_row, pltpu.roll(first, SUBLANES - 1, 0))
                window.append(nxt)
                uc = (cbias[c] + window[0] * cw[c][0] + window[1] * cw[c][1] + window[2] * cw[c][2]
                      + window[3] * cw[c][3])
                uc_scr[t * SUBLANES:(t + 1) * SUBLANES, lanes] = uc
                window.pop(0)

        for hh in range(D_LRU // MXU_K):
            cols = slice(hh * GATE_COLS, (hh + 1) * GATE_COLS)
            lhs = uc_scr[:, hh * MXU_K:(hh + 1) * MXU_K].astype(BF16)
            pre_scr[:, cols] = jnp.dot(lhs, wg_ref[hh], preferred_element_type=F32) + bg_ref[:, cols]

        segs = slice(u * SUBLANES, (u + 1) * SUBLANES)
        hs = [jnp.zeros((SUBLANES, LANES), F32) for _ in range(N_SLAB)]
        ps = [jnp.ones((SUBLANES, LANES), F32) for _ in range(N_SLAB)]
        for t in range(SEG):
            rows = slice(t * SUBLANES, (t + 1) * SUBLANES)
            for c in range(N_SLAB):
                a, b = scan_inputs(rows, c, 0)
                hs[c] = a * hs[c] + b
                ps[c] = a * ps[c]
                hf_scr[rows, slab_lanes[c]] = hs[c]
                st_ref[1, c, pl.ds(base + t, SUBLANES, stride=SEG_PITCH), :] = ps[c]
        for c in range(N_SLAB):
            ah_ref[0, segs, slab_lanes[c]] = ps[c]
            ah_ref[1, segs, slab_lanes[c]] = hs[c]

        hs = [jnp.zeros((SUBLANES, LANES), F32) for _ in range(N_SLAB)]
        ps = [jnp.ones((SUBLANES, LANES), F32) for _ in range(N_SLAB)]
        for t in reversed(range(SEG)):
            rows = slice(t * SUBLANES, (t + 1) * SUBLANES)
            for c in range(N_SLAB):
                a, b = scan_inputs(rows, c, 1)
                hs[c] = a * hs[c] + b
                ps[c] = a * ps[c]
                st_ref[0, c, pl.ds(base + t, SUBLANES, stride=SEG_PITCH), :] = hf_scr[rows, slab_lanes[c]] + hs[c]
                st_ref[2, c, pl.ds(base + t, SUBLANES, stride=SEG_PITCH), :] = ps[c]
        for c in range(N_SLAB):
            ah_ref[2, segs, slab_lanes[c]] = ps[c]
            ah_ref[3, segs, slab_lanes[c]] = hs[c]


def _lru_local(ur, layer, conv_w, conv_b, wg, bg, lam):
    b, _, rows, _ = ur.shape
    step_rows = LRU_TILES_PER_STEP * TILE_PITCH
    n_steps = rows // step_rows
    blocks_per_step = step_rows // SUBLANES
    last_block = rows // SUBLANES - 1
    s = rows // TILE_PITCH * ROW_TILE
    slab = _slab_shape(b, s)
    slab_spec = pl.BlockSpec((None, N_SLAB, step_rows, LANES), lambda bi, i: (bi, 0, i, 0))
    prev_spec = pl.BlockSpec((None, N_SLAB, SUBLANES, LANES),
                             lambda bi, i: (bi, 0, jnp.maximum(i * blocks_per_step - 2, 0), 0))
    next_spec = pl.BlockSpec((None, N_SLAB, SUBLANES, LANES),
                             lambda bi, i: (bi, 0, jnp.minimum((i + 1) * blocks_per_step, last_block), 0))
    n_seg = s // SEG
    n_half = D_LRU // MXU_K
    tile_f32 = pltpu.VMEM((ROW_TILE, D_LRU), F32)
    return pl.pallas_call(
        _lru_kernel,
        grid=(b, n_steps),
        in_specs=[
            slab_spec, prev_spec, next_spec,
            _const_spec((CONV_WIDTH, D_LRU)),
            _const_spec((1, D_LRU)),
            _layer_spec(layer, (n_half, MXU_K, GATE_COLS)),
            _const_spec((1, n_half * GATE_COLS)),
            _const_spec((2, D_LRU)),
        ],
        out_specs=[pl.BlockSpec((None, N_STATE, N_SLAB, step_rows, LANES), lambda bi, i: (bi, 0, 0, i, 0)),
                   pl.BlockSpec((None, 4, LRU_TILES_PER_STEP * SUBLANES, D_LRU), lambda bi, i: (bi, 0, i, 0))],
        out_shape=[jax.ShapeDtypeStruct((b, N_STATE) + slab.shape[1:], F32),
                   jax.ShapeDtypeStruct((b, 4, n_seg, D_LRU), F32)],
        scratch_shapes=[tile_f32, pltpu.VMEM((ROW_TILE, n_half * GATE_COLS), F32), tile_f32],
        compiler_params=_params("parallel", "parallel"),
        name="lru_local",
    )(ur, ur, ur, conv_w, conv_b.reshape(1, D_LRU), wg, bg, lam)


def _carry_kernel(ah_ref, cc_ref, *, n_seg):
    def step(n, carry):
        cf, cb = carry
        sf = n
        sb = n_seg - 1 - n
        cc_ref[0, pl.ds(sf, 1), :] = cf
        cc_ref[1, pl.ds(sb, 1), :] = cb
        return (ah_ref[1, pl.ds(sf, 1), :] + ah_ref[0, pl.ds(sf, 1), :] * cf,
                ah_ref[3, pl.ds(sb, 1), :] + ah_ref[2, pl.ds(sb, 1), :] * cb)

    zero = jnp.zeros((1, D_LRU), F32)
    lax.fori_loop(0, n_seg, step, (zero, zero), unroll=CARRY_UNROLL)


def _segment_carries(ah):
    b, _, n_seg, _ = ah.shape
    c_shape = jax.ShapeDtypeStruct((b, 2, n_seg, D_LRU), F32)
    c_spec = pl.BlockSpec((None, 2, n_seg, D_LRU), lambda bi: (bi, 0, 0, 0))
    return pl.pallas_call(
        functools.partial(_carry_kernel, n_seg=n_seg),
        grid=(b,),
        in_specs=[pl.BlockSpec((None, 4, n_seg, D_LRU), lambda bi: (bi, 0, 0, 0))],
        out_specs=c_spec,
        out_shape=c_shape,
        compiler_params=_params("parallel"),
        name="lru_carry",
    )(ah)


def _ffn_kernel(x_ref, yf_ref, gg_ref, st_ref, cc_ref, mod_ref, g_ref,
                wo_ref, w1_ref, w2_ref, gfin_ref, o_ref, yr_scr, *, final_norm):
    for c in range(N_SLAB):
        lanes = slice(c * LANES, (c + 1) * LANES)
        for j in range(ROW_TILE // SEG):
            rows = slice(j * SEG_PITCH, j * SEG_PITCH + SEG)
            hsum = (st_ref[0, c, rows, :] + st_ref[1, c, rows, :] * cc_ref[0, j:j + 1, lanes]
                    + st_ref[2, c, rows, :] * cc_ref[1, j:j + 1, lanes])
            gate = gg_ref[j * SEG:(j + 1) * SEG, lanes].astype(F32)
            yr_scr[j * SEG:(j + 1) * SEG, lanes] = (hsum * gate).astype(BF16)
    gate1 = mod_ref[:, 2 * D_MODEL:3 * D_MODEL]
    shift2 = mod_ref[:, 3 * D_MODEL:4 * D_MODEL]
    scale2 = mod_ref[:, 4 * D_MODEL:5 * D_MODEL]
    gate2 = mod_ref[:, 5 * D_MODEL:6 * D_MODEL]
    mix = (jnp.dot(yf_ref[...], wo_ref[0:D_FOURIER, :], preferred_element_type=F32)
           + jnp.dot(yr_scr[...], wo_ref[D_FOURIER:, :], preferred_element_type=F32))
    x1 = x_ref[...] + gate1 * mix
    h = (_rms_norm(x1, g_ref[...] * (1.0 + scale2)) + shift2).astype(BF16)
    acc = jnp.zeros((ROW_TILE, D_MODEL), F32)
    for k in range(D_FF // FF_CHUNK):
        cols = slice(k * FF_CHUNK, (k + 1) * FF_CHUNK)
        t = jnp.maximum(jnp.dot(h, w1_ref[:, cols], preferred_element_type=F32), 0.0)
        acc = acc + jnp.dot((t * t).astype(BF16), w2_ref[cols, :], preferred_element_type=F32)
    x2 = x1 + gate2 * acc
    if final_norm:
        x2 = _rms_norm(x2, gfin_ref[...])
    o_ref[...] = x2


def _out_and_mlp(x, yf, gg, st, cc, mod, layer, row0, g_mlp, w_out, w_ff1, w_ff2, g_final,
                 final_norm):
    b, s, _ = x.shape
    segs = ROW_TILE // SEG
    x_spec = pl.BlockSpec((None, ROW_TILE, D_MODEL), lambda bi, i: (bi, i, 0))
    st_spec = pl.BlockSpec((None, N_STATE, N_SLAB, TILE_PITCH, LANES), lambda bi, i: (bi, 0, 0, i, 0))
    c_spec = pl.BlockSpec((None, 2, segs, D_LRU), lambda bi, i: (bi, 0, i, 0))
    return pl.pallas_call(
        functools.partial(_ffn_kernel, final_norm=final_norm),
        grid=(b, s // ROW_TILE),
        in_specs=[
            x_spec,
            pl.BlockSpec((None, ROW_TILE, D_FOURIER), lambda bi, i: (bi, i, 0)),
            pl.BlockSpec((None, ROW_TILE, D_LRU), lambda bi, i: (bi, i, 0)),
            st_spec, c_spec,
            _mod_spec(layer, row0),
            _const_spec((1, D_MODEL)),
            _layer_spec(layer, (D_MODEL, D_MODEL)),
            _layer_spec(layer, (D_MODEL, D_FF)),
            _layer_spec(layer, (D_FF, D_MODEL)),
            _const_spec((1, D_MODEL)),
        ],
        out_specs=x_spec,
        out_shape=jax.ShapeDtypeStruct(x.shape, F32),
        scratch_shapes=[pltpu.VMEM((ROW_TILE, D_LRU), BF16)],
        compiler_params=_params("parallel", "parallel"),
        name="outproj_mlp",
    )(x, yf, gg, st, cc, mod, g_mlp.reshape(1, D_MODEL), w_out, w_ff1, w_ff2,
      g_final.reshape(1, D_MODEL))


def _gate_weights(w_rgate, b_rgate, w_igate, b_igate):
    depth = w_rgate.shape[0]
    n_half = D_LRU // MXU_K
    heads_per_half = N_HEADS // n_half
    w_halves, b_halves = [], []
    for hh in range(n_half):
        hs = slice(hh * heads_per_half, (hh + 1) * heads_per_half)
        w_cols, b_cols = [], []
        for d in range(2):
            for w, bias in ((w_rgate, b_rgate), (w_igate, b_igate)):
                w_cols.append(_block_diag(w[:, d, hs]))
                b_cols.append(bias[:, d, hs].reshape(depth, MXU_K))
        w_halves.append(jnp.concatenate(w_cols, axis=-1))
        b_halves.append(jnp.concatenate(b_cols, axis=-1))
    wg = jnp.stack(w_halves, axis=1).astype(BF16)
    bg = (0.5 * jnp.concatenate(b_halves, axis=-1)).reshape(depth, 1, n_half * GATE_COLS)
    return wg, bg


def _trunk(x, mod, row0, layers, g_final):
    depth = len(layers)
    for l, p in enumerate(layers):
        ua, uv, ur, gg = _in_projection(x, mod, l, row0, p["g_mix"], p["w_fold"])
        yf = _fourier_mix(ua, uv, x.shape[1])
        st, ah = _lru_local(ur, l, p["conv_w"], p["conv_b"], p["wg"], p["bg"], p["lam"])
        cc = _segment_carries(ah)
        x = _out_and_mlp(x, yf, gg, st, cc, mod, l, row0, p["g_mlp"], p["w_out"],
                         p["w_ff1"], p["w_ff2"], g_final, final_norm=(l == depth - 1))
    return x


def kernel(x_prompt, x_sample, c_prompt, c_sample, g_mix, g_mlp, w_mod, b_mod, w_in, w_fourier,
           conv_w, conv_b, w_rgate, b_rgate, w_igate, b_igate, lru_lambda, w_out, w_ff1, w_ff2,
           g_final):
    depth = w_in.shape[0]
    n_prompt = c_prompt.shape[0]
    n_cond = n_prompt + c_sample.shape[0]
    pad = -n_cond % SUBLANES
    c_all = jnp.concatenate([c_prompt, c_sample, jnp.zeros((pad, D_MODEL), F32)], axis=0)
    mod = _modulation(c_all, w_mod, b_mod)
    mod = mod.reshape(depth, n_cond + pad, 1, N_MOD * D_MODEL)

    w_fold = _fold_in_weights(w_in, w_fourier)
    wg, bg = _gate_weights(w_rgate, b_rgate, w_igate, b_igate)
    w_out_b = w_out.astype(BF16)
    w_ff1_b = w_ff1.astype(BF16)
    w_ff2_b = w_ff2.astype(BF16)
    layers = [dict(g_mix=g_mix[l], g_mlp=g_mlp[l], w_fold=w_fold, conv_w=conv_w[l], conv_b=conv_b[l],
                   wg=wg, bg=bg[l], lam=lru_lambda[l], w_out=w_out_b, w_ff1=w_ff1_b,
                   w_ff2=w_ff2_b) for l in range(depth)]
    y_prompt = _trunk(x_prompt, mod, 0, layers, g_final)
    y_sample = _trunk(x_sample, mod, n_prompt, layers, g_final)
    return (y_prompt, y_sample)
```

```python
import functools
import math

import numpy as np
import jax
import jax.numpy as jnp
from jax import lax
from jax.experimental import pallas as pl
from jax.experimental.pallas import tpu as pltpu

F32 = jnp.float32
BF16 = jnp.bfloat16

D_MODEL = 1024
HEAD_DIM = 64
D_FOURIER = 512
D_LRU = 512
N_HEADS = 8
D_FF = 4096
N_MOD = 6
EPS = 1e-6
LRU_C = 8.0
CONV_WIDTH = 4

LANES = 128
SUBLANES = 8
BF16_ROWS = 16
N_SLAB = D_LRU // LANES
ROW_TILE = 512
IN_TILE = 1024
SEG = ROW_TILE // SUBLANES
SEG_PITCH = SEG + SUBLANES
TILE_PITCH = SUBLANES * SEG_PITCH
LRU_TILES_PER_STEP = 4
N_STATE = 3
LOG2E = 1.4426950408889634
COL_G, COL_R, COL_A, COL_V = 0, D_LRU, 2 * D_LRU, 2 * D_LRU + D_FOURIER
N_STATS = 4
N_DIR = 2
MXU_K = 256
GATE_COLS = 4 * MXU_K
DFT_N2 = 256
DFT_GROUP = BF16_ROWS
DFT_ROWS_PER_STEP = 1024
CARRY_UNROLL = 4
FF_CHUNK = 1024
MOD_COLS = 1536
VMEM_LIMIT = 56 * 1024 * 1024


def _const_spec(shape):
    n = len(shape)
    return pl.BlockSpec(shape, lambda *_: (0,) * n, pipeline_mode=pl.Buffered(1))


def _layer_spec(layer, shape):
    n = len(shape)
    return pl.BlockSpec((None,) + tuple(shape), lambda *_: (layer,) + (0,) * n, pipeline_mode=pl.Buffered(1))


def _dot3(a, b):
    a_hi = a.astype(BF16)
    b_hi = b.astype(BF16)
    a_lo = (a - a_hi.astype(F32)).astype(BF16)
    b_lo = (b - b_hi.astype(F32)).astype(BF16)
    return (jnp.dot(a_hi, b_hi, preferred_element_type=F32) + jnp.dot(a_hi, b_lo, preferred_element_type=F32)
            + jnp.dot(a_lo, b_hi, preferred_element_type=F32))


def _params(*sem):
    return pltpu.CompilerParams(dimension_semantics=sem, vmem_limit_bytes=VMEM_LIMIT)


def _mod_kernel(c_ref, w_ref, b_ref, o_ref):
    c = c_ref[...]
    s = c * jax.nn.sigmoid(c)
    o_ref[...] = _dot3(s, w_ref[...]) + b_ref[...]


def _modulation(c_all, w_mod, b_mod):
    depth = w_mod.shape[0]
    n_out = N_MOD * D_MODEL
    rows = c_all.shape[0]
    return pl.pallas_call(
        _mod_kernel,
        grid=(depth, n_out // MOD_COLS),
        in_specs=[
            pl.BlockSpec((rows, D_MODEL), lambda l, j: (0, 0)),
            pl.BlockSpec((None, D_MODEL, MOD_COLS), lambda l, j: (l, 0, j)),
            pl.BlockSpec((None, 1, MOD_COLS), lambda l, j: (l, 0, j)),
        ],
        out_specs=pl.BlockSpec((None, rows, MOD_COLS), lambda l, j: (l, 0, j)),
        out_shape=jax.ShapeDtypeStruct((depth, rows, n_out), F32),
        compiler_params=_params("parallel", "parallel"),
        name="adaln_mod",
    )(c_all, w_mod, b_mod.reshape(depth, 1, n_out))


def _prep_kernel(win_ref, bdwf_ref, bdc_ref, bds_ref, o_ref):
    for blk in range(D_FOURIER // MXU_K):
        sl = slice(blk * MXU_K, (blk + 1) * MXU_K)
        wf = bdwf_ref[sl, sl]
        ma = _dot3(bdc_ref[sl, sl], wf)
        mv = -_dot3(bds_ref[sl, sl], wf)
        wi = win_ref[:, sl]
        o_ref[:, COL_A + blk * MXU_K:COL_A + (blk + 1) * MXU_K] = _dot3(wi, ma).astype(BF16)
        o_ref[:, COL_V + blk * MXU_K:COL_V + (blk + 1) * MXU_K] = _dot3(wi, mv).astype(BF16)
    o_ref[:, COL_R:COL_R + D_LRU] = win_ref[:, D_FOURIER:D_FOURIER + D_LRU].astype(BF16)
    o_ref[:, COL_G:COL_G + D_LRU] = win_ref[:, D_FOURIER + D_LRU:].astype(BF16)


def _block_diag(w):
    h, d, e = w.shape[-3:]
    eye = jnp.eye(h, dtype=w.dtype)
    out = w[..., :, :, None, :] * eye[:, None, :, None]
    return out.reshape(w.shape[:-3] + (h * d, h * e))


def _channel_dft_tables():
    k = np.arange(HEAD_DIM)
    ang = 2.0 * np.pi * np.outer(k, k) / HEAD_DIM
    eye = np.eye(N_HEADS)
    return (jnp.asarray(np.kron(eye, np.cos(ang)), F32), jnp.asarray(np.kron(eye, np.sin(ang)), F32))


def _fold_in_weights(w_in, w_fourier):
    depth = w_in.shape[0]
    d_in = w_in.shape[2]
    n_out = d_in + D_FOURIER
    bdc, bds = _channel_dft_tables()
    bdwf = _block_diag(w_fourier)
    return pl.pallas_call(
        _prep_kernel,
        grid=(depth,),
        in_specs=[
            pl.BlockSpec((None, D_MODEL, d_in), lambda l: (l, 0, 0)),
            pl.BlockSpec((None, D_FOURIER, D_FOURIER), lambda l: (l, 0, 0)),
            _const_spec((D_FOURIER, D_FOURIER)),
            _const_spec((D_FOURIER, D_FOURIER)),
        ],
        out_specs=pl.BlockSpec((None, D_MODEL, n_out), lambda l: (l, 0, 0)),
        out_shape=jax.ShapeDtypeStruct((depth, D_MODEL, n_out), BF16),
        compiler_params=_params("parallel"),
        name="fold_in_weights",
    )(w_in, bdwf, bdc, bds)


def _rms_norm(x, g):
    ms = jnp.mean(x * x, axis=-1, keepdims=True)
    return x * lax.rsqrt(ms + EPS) * g


def _gelu_tanh(x):
    c = math.sqrt(2.0 / math.pi)
    return 0.5 * x * (1.0 + jnp.tanh(c * (x + 0.044715 * (x * x * x))))


def _mod_spec(layer, row0):
    return pl.BlockSpec((None, None, 1, N_MOD * D_MODEL), lambda b, i: (layer, row0 + b, 0, 0))


def _slab_shape(b, s):
    return jax.ShapeDtypeStruct((b, N_SLAB, s // ROW_TILE * TILE_PITCH, LANES), F32)


def _inproj_kernel(x_ref, mod_ref, g_ref, w_ref, ua_ref, uv_ref, ur_ref, gg_ref, *, n1):
    shift = mod_ref[:, 0:D_MODEL]
    scale = mod_ref[:, D_MODEL:2 * D_MODEL]
    h = _rms_norm(x_ref[...], g_ref[...] * (1.0 + scale)) + shift
    z = jnp.dot(h.astype(BF16), w_ref[...], preferred_element_type=F32)
    n_grp = n1 // SUBLANES
    for c in range(N_SLAB):
        for s2 in range(IN_TILE // n1):
            for grp in range(n_grp):
                src = slice(s2 * n1 + grp * SUBLANES, s2 * n1 + (grp + 1) * SUBLANES)
                dst = slice(s2 * SUBLANES, (s2 + 1) * SUBLANES)
                ua_ref[c, grp, dst, :] = z[src, COL_A + c * LANES:COL_A + (c + 1) * LANES]
                uv_ref[c, grp, dst, :] = z[src, COL_V + c * LANES:COL_V + (c + 1) * LANES]
    pad = jnp.zeros((SEG_PITCH - SEG, LANES), F32)
    for c in range(N_SLAB):
        ur = z[:, COL_R + c * LANES:COL_R + (c + 1) * LANES]
        for j in range(IN_TILE // SEG):
            ur_ref[c, j * SEG_PITCH:j * SEG_PITCH + SEG, :] = ur[j * SEG:(j + 1) * SEG]
            ur_ref[c, j * SEG_PITCH + SEG:(j + 1) * SEG_PITCH, :] = pad
    gg_ref[...] = _gelu_tanh(z[:, COL_G:COL_G + D_LRU]).astype(BF16)


def _in_projection(x, mod, layer, row0, g_mix, w_fold):
    b, s, _ = x.shape
    n_out = w_fold.shape[-1]
    slab = _slab_shape(b, s)
    slab_spec = pl.BlockSpec((None, N_SLAB, IN_TILE // SEG * SEG_PITCH, LANES), lambda bi, i: (bi, 0, i, 0))
    n1 = s // DFT_N2
    n_grp = n1 // SUBLANES
    grp_rows = IN_TILE // n1 * SUBLANES
    half = jax.ShapeDtypeStruct((b, N_SLAB, n_grp, DFT_N2 * SUBLANES, LANES), F32)
    half_spec = pl.BlockSpec((None, N_SLAB, n_grp, grp_rows, LANES), lambda bi, i: (bi, 0, 0, i, 0))
    return pl.pallas_call(
        functools.partial(_inproj_kernel, n1=n1),
        grid=(b, s // IN_TILE),
        in_specs=[
            pl.BlockSpec((None, IN_TILE, D_MODEL), lambda bi, i: (bi, i, 0)),
            _mod_spec(layer, row0),
            _const_spec((1, D_MODEL)),
            _layer_spec(layer, (D_MODEL, n_out)),
        ],
        out_specs=[half_spec, half_spec, slab_spec,
                   pl.BlockSpec((None, IN_TILE, D_LRU), lambda bi, i: (bi, i, 0))],
        out_shape=[half, half, slab, jax.ShapeDtypeStruct((b, s, D_LRU), BF16)],
        compiler_params=_params("parallel", "parallel"),
        name="inproj",
    )(x, mod, g_mix.reshape(1, D_MODEL), w_fold)


def _dft_tables(seq):
    n2 = DFT_N2
    n1 = seq // n2
    k2 = np.arange(n2)
    ang = 2.0 * np.pi * np.outer(k2, k2) / n2
    c2 = np.cos(ang) / math.sqrt(n2)
    s2 = np.sin(ang) / math.sqrt(n2)
    fa = np.concatenate([c2, -s2], axis=0)
    fv = np.concatenate([s2, c2], axis=0)
    g = DFT_GROUP
    n_groups = n2 // g
    rows = n1 * g
    k1 = np.arange(n1)
    s1 = np.arange(n1)
    kp = np.arange(g)
    scale = 1.0 / math.sqrt(n1 * HEAD_DIM)
    lc = np.zeros((n_groups, n1, g, n1, g), np.float64)
    ls = np.zeros((n_groups, n1, g, n1, g), np.float64)
    for grp in range(n_groups):
        k = n2 * k1[:, None] + (g * grp + kp)[None, :]
        ang_b = 2.0 * np.pi * (k[:, :, None] * s1[None, None, :] % seq) / seq
        for q in range(g):
            lc[grp, :, q, :, q] = np.cos(ang_b[:, q, :]) * scale
            ls[grp, :, q, :, q] = np.sin(ang_b[:, q, :]) * scale
    lc = lc.reshape(n_groups, rows, rows)
    ls = ls.reshape(n_groups, rows, rows)
    as_bf16 = lambda a: jnp.asarray(a.astype(np.float32)).astype(BF16)
    return as_bf16(fa), as_bf16(fv), as_bf16(lc), as_bf16(ls)


def _dft_kernel(a_ref, v_ref, fa_ref, fv_ref, lc_ref, ls_ref, y_ref, tre_scr, tim_scr, *, n1, groups_per_step):
    k = pl.program_id(1)
    n_grp = n1 // SUBLANES

    @pl.when(k < n_grp)
    def _():
        for q in range(SUBLANES):
            rows = pl.ds(q, DFT_N2, stride=SUBLANES)
            a = jnp.concatenate([a_ref[c, rows, :] for c in range(N_SLAB)], axis=1).astype(BF16)
            v = jnp.concatenate([v_ref[c, rows, :] for c in range(N_SLAB)], axis=1).astype(BF16)
            o = (jnp.dot(fa_ref[...], a, preferred_element_type=F32)
                 + jnp.dot(fv_ref[...], v, preferred_element_type=F32))
            tre_scr[k * SUBLANES + q] = o[0:DFT_N2].astype(BF16)
            tim_scr[k * SUBLANES + q] = o[DFT_N2:].astype(BF16)

    @pl.when(k >= n_grp)
    def _():
        rows = n1 * DFT_GROUP
        for j in range(groups_per_step):
            first = pl.multiple_of(((k - n_grp) * groups_per_step + j) * DFT_GROUP, DFT_GROUP)
            tr = tre_scr[:, pl.ds(first, DFT_GROUP), :].reshape(rows, D_FOURIER)
            ti = tim_scr[:, pl.ds(first, DFT_GROUP), :].reshape(rows, D_FOURIER)
            o = (jnp.dot(lc_ref[j], tr, preferred_element_type=F32)
                 + jnp.dot(ls_ref[j], ti, preferred_element_type=F32))
            y_ref[:, j * DFT_GROUP:(j + 1) * DFT_GROUP, :] = o.astype(BF16).reshape(n1, DFT_GROUP, D_FOURIER)


def _fourier_mix(ua, uv, s):
    b = ua.shape[0]
    n2 = DFT_N2
    n1 = s // n2
    n_grp = n1 // SUBLANES
    rows = n1 * DFT_GROUP
    groups_per_step = DFT_ROWS_PER_STEP // rows
    n_out = n2 // (DFT_GROUP * groups_per_step)
    fa, fv, lc, ls = _dft_tables(s)
    in_spec = pl.BlockSpec((None, N_SLAB, None, n2 * SUBLANES, LANES),
                           lambda bi, k: (bi, 0, jnp.minimum(k, n_grp - 1), 0, 0))
    l_spec = pl.BlockSpec((groups_per_step, rows, rows), lambda bi, k: (jnp.maximum(k - n_grp, 0), 0, 0))
    t_scr = pltpu.VMEM((n1, n2, D_FOURIER), BF16)
    y = pl.pallas_call(
        functools.partial(_dft_kernel, n1=n1, groups_per_step=groups_per_step),
        grid=(b, n_grp + n_out),
        in_specs=[in_spec, in_spec, _const_spec((2 * n2, n2)), _const_spec((2 * n2, n2)), l_spec, l_spec],
        out_specs=pl.BlockSpec((None, n1, DFT_GROUP * groups_per_step, D_FOURIER),
                               lambda bi, k: (bi, 0, jnp.maximum(k - n_grp, 0), 0)),
        out_shape=jax.ShapeDtypeStruct((b, n1, n2, D_FOURIER), BF16),
        scratch_shapes=[t_scr, t_scr],
        compiler_params=_params("parallel", "arbitrary"),
        name="seq_dft",
    )(ua, uv, fa, fv, lc, ls)
    return y.reshape(b, s, D_FOURIER)


def _lru_kernel(ur_ref, prev_ref, next_ref, cw_ref, cb_ref, wg_ref, bg_ref, lam_ref,
                st_ref, ah_ref, uc_scr, pre_scr, hf_scr):
    i = pl.program_id(1)
    n_steps = pl.num_programs(1)
    sub = lax.broadcasted_iota(jnp.int32, (SUBLANES, LANES), 0)
    slab_lanes = [slice(c * LANES, (c + 1) * LANES) for c in range(N_SLAB)]

    cw = [[jnp.broadcast_to(0.5 * cw_ref[k:k + 1, slab_lanes[c]], (SUBLANES, LANES)) for k in range(CONV_WIDTH)]
          for c in range(N_SLAB)]
    cbias = [jnp.broadcast_to(0.5 * cb_ref[:, slab_lanes[c]], (SUBLANES, LANES)) for c in range(N_SLAB)]
    neg_lam = -lam_ref[...]
    softplus = jnp.maximum(neg_lam, 0.0) + jnp.log1p(jnp.exp(-jnp.abs(neg_lam)))
    k_all = (-0.5 * LRU_C * LOG2E) * softplus
    k_rows = [[jnp.broadcast_to(k_all[d:d + 1, slab_lanes[c]], (SUBLANES, LANES)) for c in range(N_SLAB)]
              for d in range(2)]
    pad = jnp.zeros((SEG_PITCH - SEG, LANES), F32)

    def scan_inputs(rows, c, d):
        col = (c * LANES // MXU_K) * GATE_COLS + d * 2 * MXU_K + (c * LANES) % MXU_K
        th_r = jnp.tanh(pre_scr[rows, col:col + LANES])
        th_i = jnp.tanh(pre_scr[rows, col + MXU_K:col + MXU_K + LANES])
        k = k_rows[d][c]
        a = jnp.exp2(k * th_r + k)
        m = 1.0 - a * a
        mult = jnp.where(m > 0.0, m * lax.rsqrt(m), 0.0)
        return a, mult * ((th_i + 1.0) * uc_scr[rows, slab_lanes[c]])

    for u in range(LRU_TILES_PER_STEP):
        base = u * TILE_PITCH
        last_seg_end = base - SEG_PITCH + SEG

        def segment_rows(ref, c, t, base=base):
            return ref[c, pl.ds(base + t, SUBLANES, stride=SEG_PITCH), :]

        for c in range(N_SLAB):
            for j in range(SUBLANES):
                pad_rows = slice(base + j * SEG_PITCH + SEG, base + (j + 1) * SEG_PITCH)
                for plane in range(N_STATE):
                    st_ref[plane, c, pad_rows, :] = pad

        for c in range(N_SLAB):
            lanes = slab_lanes[c]
            if u == 0:
                prev_rows = [jnp.where(i > 0, prev_ref[c, SUBLANES - 2 + q:SUBLANES - 1 + q, :], 0.0) for q in range(2)]
            else:
                prev_rows = [ur_ref[c, last_seg_end - 2 + q:last_seg_end - 1 + q, :] for q in range(2)]
            if u == LRU_TILES_PER_STEP - 1:
                next_row = jnp.where(i < n_steps - 1, next_ref[c, 0:1, :], 0.0)
            else:
                next_row = ur_ref[c, base + TILE_PITCH:base + TILE_PITCH + 1, :]
            first = segment_rows(ur_ref, c, 0)
            window = [jnp.where(sub == 0, prev_rows[q], pltpu.roll(segment_rows(ur_ref, c, SEG - 2 + q), 1, 0))
                      for q in range(2)] + [first]
            for t in range(SEG):
                if t + 1 < SEG:
                    nxt = segment_rows(ur_ref, c, t + 1)
                else:
                    nxt = jnp.where(sub == SUBLANES - 1, next_row, pltpu.roll(first, SUBLANES - 1, 0))
                window.append(nxt)
                uc = (cbias[c] + window[0] * cw[c][0] + window[1] * cw[c][1] + window[2] * cw[c][2]
                      + window[3] * cw[c][3])
                uc_scr[t * SUBLANES:(t + 1) * SUBLANES, lanes] = uc
                window.pop(0)

        for hh in range(D_LRU // MXU_K):
            cols = slice(hh * GATE_COLS, (hh + 1) * GATE_COLS)
            lhs = uc_scr[:, hh * MXU_K:(hh + 1) * MXU_K].astype(BF16)
            pre_scr[:, cols] = jnp.dot(lhs, wg_ref[hh], preferred_element_type=F32) + bg_ref[:, cols]

        segs = slice(u * SUBLANES, (u + 1) * SUBLANES)
        hs = [jnp.zeros((SUBLANES, LANES), F32) for _ in range(N_SLAB)]
        ps = [jnp.ones((SUBLANES, LANES), F32) for _ in range(N_SLAB)]
        for t in range(SEG):
            rows = slice(t * SUBLANES, (t + 1) * SUBLANES)
            for c in range(N_SLAB):
                a, b = scan_inputs(rows, c, 0)
                hs[c] = a * hs[c] + b
                ps[c] = a * ps[c]
                hf_scr[rows, slab_lanes[c]] = hs[c]
                st_ref[1, c, pl.ds(base + t, SUBLANES, stride=SEG_PITCH), :] = ps[c]
        for c in range(N_SLAB):
            ah_ref[0, segs, slab_lanes[c]] = ps[c]
            ah_ref[1, segs, slab_lanes[c]] = hs[c]

        hs = [jnp.zeros((SUBLANES, LANES), F32) for _ in range(N_SLAB)]
        ps = [jnp.ones((SUBLANES, LANES), F32) for _ in range(N_SLAB)]
        for t in reversed(range(SEG)):
            rows = slice(t * SUBLANES, (t + 1) * SUBLANES)
            for c in range(N_SLAB):
                a, b = scan_inputs(rows, c, 1)
                hs[c] = a * hs[c] + b
                ps[c] = a * ps[c]
                st_ref[0, c, pl.ds(base + t, SUBLANES, stride=SEG_PITCH), :] = hf_scr[rows, slab_lanes[c]] + hs[c]
                st_ref[2, c, pl.ds(base + t, SUBLANES, stride=SEG_PITCH), :] = ps[c]
        for c in range(N_SLAB):
            ah_ref[2, segs, slab_lanes[c]] = ps[c]
            ah_ref[3, segs, slab_lanes[c]] = hs[c]


def _lru_local(ur, layer, conv_w, conv_b, wg, bg, lam):
    b, _, rows, _ = ur.shape
    step_rows = LRU_TILES_PER_STEP * TILE_PITCH
    n_steps = rows // step_rows
    blocks_per_step = step_rows // SUBLANES
    last_block = rows // SUBLANES - 1
    s = rows // TILE_PITCH * ROW_TILE
    slab = _slab_shape(b, s)
    slab_spec = pl.BlockSpec((None, N_SLAB, step_rows, LANES), lambda bi, i: (bi, 0, i, 0))
    prev_spec = pl.BlockSpec((None, N_SLAB, SUBLANES, LANES),
                             lambda bi, i: (bi, 0, jnp.maximum(i * blocks_per_step - 2, 0), 0))
    next_spec = pl.BlockSpec((None, N_SLAB, SUBLANES, LANES),
                             lambda bi, i: (bi, 0, jnp.minimum((i + 1) * blocks_per_step, last_block), 0))
    n_seg = s // SEG
    n_half = D_LRU // MXU_K
    tile_f32 = pltpu.VMEM((ROW_TILE, D_LRU), F32)
    return pl.pallas_call(
        _lru_kernel,
        grid=(b, n_steps),
        in_specs=[
            slab_spec, prev_spec, next_spec,
            _const_spec((CONV_WIDTH, D_LRU)),
            _const_spec((1, D_LRU)),
            _layer_spec(layer, (n_half, MXU_K, GATE_COLS)),
            _const_spec((1, n_half * GATE_COLS)),
            _const_spec((2, D_LRU)),
        ],
        out_specs=[pl.BlockSpec((None, N_STATE, N_SLAB, step_rows, LANES), lambda bi, i: (bi, 0, 0, i, 0)),
                   pl.BlockSpec((None, N_STATS, LRU_TILES_PER_STEP * SUBLANES, D_LRU),
                                lambda bi, i: (bi, 0, i, 0))],
        out_shape=[jax.ShapeDtypeStruct((b, N_STATE) + slab.shape[1:], F32),
                   jax.ShapeDtypeStruct((b, N_STATS, n_seg, D_LRU), F32)],
        scratch_shapes=[tile_f32, pltpu.VMEM((ROW_TILE, n_half * GATE_COLS), F32), tile_f32],
        compiler_params=_params("parallel", "parallel"),
        name="lru_local",
    )(ur, ur, ur, conv_w, conv_b.reshape(1, D_LRU), wg, bg, lam)


def _carry_kernel(ah_ref, cc_ref, *, n_seg):
    def step(n, carry):
        cf, cb = carry
        sf = n
        sb = n_seg - 1 - n
        cc_ref[0, pl.ds(sf, 1), :] = cf
        cc_ref[1, pl.ds(sb, 1), :] = cb
        return (ah_ref[1, pl.ds(sf, 1), :] + ah_ref[0, pl.ds(sf, 1), :] * cf,
                ah_ref[3, pl.ds(sb, 1), :] + ah_ref[2, pl.ds(sb, 1), :] * cb)

    zero = jnp.zeros((1, D_LRU), F32)
    lax.fori_loop(0, n_seg, step, (zero, zero), unroll=CARRY_UNROLL)


def _segment_carries(ah):
    b, _, n_seg, _ = ah.shape
    c_shape = jax.ShapeDtypeStruct((b, N_DIR, n_seg, D_LRU), F32)
    c_spec = pl.BlockSpec((None, N_DIR, n_seg, D_LRU), lambda bi: (bi, 0, 0, 0))
    return pl.pallas_call(
        functools.partial(_carry_kernel, n_seg=n_seg),
        grid=(b,),
        in_specs=[pl.BlockSpec((None, N_STATS, n_seg, D_LRU), lambda bi: (bi, 0, 0, 0))],
        out_specs=c_spec,
        out_shape=c_shape,
        compiler_params=_params("parallel"),
        name="lru_carry",
    )(ah)


def _ffn_kernel(x_ref, yf_ref, gg_ref, st_ref, cc_ref, mod_ref, g_ref,
                wo_ref, w1_ref, w2_ref, gfin_ref, o_ref, yr_scr, *, final_norm):
    for c in range(N_SLAB):
        lanes = slice(c * LANES, (c + 1) * LANES)
        for j in range(ROW_TILE // SEG):
            rows = slice(j * SEG_PITCH, j * SEG_PITCH + SEG)
            hsum = (st_ref[0, c, rows, :] + st_ref[1, c, rows, :] * cc_ref[0, j:j + 1, lanes]
                    + st_ref[2, c, rows, :] * cc_ref[1, j:j + 1, lanes])
            gate = gg_ref[j * SEG:(j + 1) * SEG, lanes].astype(F32)
            yr_scr[j * SEG:(j + 1) * SEG, lanes] = (hsum * gate).astype(BF16)
    gate1 = mod_ref[:, 2 * D_MODEL:3 * D_MODEL]
    shift2 = mod_ref[:, 3 * D_MODEL:4 * D_MODEL]
    scale2 = mod_ref[:, 4 * D_MODEL:5 * D_MODEL]
    gate2 = mod_ref[:, 5 * D_MODEL:6 * D_MODEL]
    mix = (jnp.dot(yf_ref[...], wo_ref[0:D_FOURIER, :], preferred_element_type=F32)
           + jnp.dot(yr_scr[...], wo_ref[D_FOURIER:, :], preferred_element_type=F32))
    x1 = x_ref[...] + gate1 * mix
    h = (_rms_norm(x1, g_ref[...] * (1.0 + scale2)) + shift2).astype(BF16)
    acc = jnp.zeros((ROW_TILE, D_MODEL), F32)
    for k in range(D_FF // FF_CHUNK):
        cols = slice(k * FF_CHUNK, (k + 1) * FF_CHUNK)
        t = jnp.maximum(jnp.dot(h, w1_ref[:, cols], preferred_element_type=F32), 0.0)
        acc = acc + jnp.dot((t * t).astype(BF16), w2_ref[cols, :], preferred_element_type=F32)
    x2 = x1 + gate2 * acc
    if final_norm:
        x2 = _rms_norm(x2, gfin_ref[...])
    o_ref[...] = x2


def _out_and_mlp(x, yf, gg, st, cc, mod, layer, row0, g_mlp, w_out, w_ff1, w_ff2, g_final,
                 final_norm):
    b, s, _ = x.shape
    segs = ROW_TILE // SEG
    x_spec = pl.BlockSpec((None, ROW_TILE, D_MODEL), lambda bi, i: (bi, i, 0))
    st_spec = pl.BlockSpec((None, N_STATE, N_SLAB, TILE_PITCH, LANES), lambda bi, i: (bi, 0, 0, i, 0))
    c_spec = pl.BlockSpec((None, N_DIR, segs, D_LRU), lambda bi, i: (bi, 0, i, 0))
    return pl.pallas_call(
        functools.partial(_ffn_kernel, final_norm=final_norm),
        grid=(b, s // ROW_TILE),
        in_specs=[
            x_spec,
            pl.BlockSpec((None, ROW_TILE, D_FOURIER), lambda bi, i: (bi, i, 0)),
            pl.BlockSpec((None, ROW_TILE, D_LRU), lambda bi, i: (bi, i, 0)),
            st_spec, c_spec,
            _mod_spec(layer, row0),
            _const_spec((1, D_MODEL)),
            _layer_spec(layer, (D_MODEL, D_MODEL)),
            _layer_spec(layer, (D_MODEL, D_FF)),
            _layer_spec(layer, (D_FF, D_MODEL)),
            _const_spec((1, D_MODEL)),
        ],
        out_specs=x_spec,
        out_shape=jax.ShapeDtypeStruct(x.shape, F32),
        scratch_shapes=[pltpu.VMEM((ROW_TILE, D_LRU), BF16)],
        compiler_params=_params("parallel", "parallel"),
        name="outproj_mlp",
    )(x, yf, gg, st, cc, mod, g_mlp.reshape(1, D_MODEL), w_out, w_ff1, w_ff2,
      g_final.reshape(1, D_MODEL))


def _gate_weights(w_rgate, b_rgate, w_igate, b_igate):
    depth = w_rgate.shape[0]
    n_half = D_LRU // MXU_K
    heads_per_half = N_HEADS // n_half
    w_halves, b_halves = [], []
    for hh in range(n_half):
        hs = slice(hh * heads_per_half, (hh + 1) * heads_per_half)
        w_cols, b_cols = [], []
        for d in range(2):
            for w, bias in ((w_rgate, b_rgate), (w_igate, b_igate)):
                w_cols.append(_block_diag(w[:, d, hs]))
                b_cols.append(bias[:, d, hs].reshape(depth, MXU_K))
        w_halves.append(jnp.concatenate(w_cols, axis=-1))
        b_halves.append(jnp.concatenate(b_cols, axis=-1))
    wg = jnp.stack(w_halves, axis=1).astype(BF16)
    bg = (0.5 * jnp.concatenate(b_halves, axis=-1)).reshape(depth, 1, n_half * GATE_COLS)
    return wg, bg


def _trunk(x, mod, row0, layers, g_final):
    depth = len(layers)
    for l, p in enumerate(layers):
        ua, uv, ur, gg = _in_projection(x, mod, l, row0, p["g_mix"], p["w_fold"])
        yf = _fourier_mix(ua, uv, x.shape[1])
        st, ah = _lru_local(ur, l, p["conv_w"], p["conv_b"], p["wg"], p["bg"], p["lam"])
        cc = _segment_carries(ah)
        x = _out_and_mlp(x, yf, gg, st, cc, mod, l, row0, p["g_mlp"], p["w_out"],
                         p["w_ff1"], p["w_ff2"], g_final, final_norm=(l == depth - 1))
    return x


def kernel(x_prompt, x_sample, c_prompt, c_sample, g_mix, g_mlp, w_mod, b_mod, w_in, w_fourier,
           conv_w, conv_b, w_rgate, b_rgate, w_igate, b_igate, lru_lambda, w_out, w_ff1, w_ff2,
           g_final):
    depth = w_in.shape[0]
    n_prompt = c_prompt.shape[0]
    n_cond = n_prompt + c_sample.shape[0]
    pad = -n_cond % SUBLANES
    c_all = jnp.concatenate([c_prompt, c_sample, jnp.zeros((pad, D_MODEL), F32)], axis=0)
    mod = _modulation(c_all, w_mod, b_mod)
    mod = mod.reshape(depth, n_cond + pad, 1, N_MOD * D_MODEL)

    w_fold = _fold_in_weights(w_in, w_fourier)
    wg, bg = _gate_weights(w_rgate, b_rgate, w_igate, b_igate)
    w_out_b = w_out.astype(BF16)
    w_ff1_b = w_ff1.astype(BF16)
    w_ff2_b = w_ff2.astype(BF16)
    layers = [dict(g_mix=g_mix[l], g_mlp=g_mlp[l], w_fold=w_fold, conv_w=conv_w[l], conv_b=conv_b[l],
                   wg=wg, bg=bg[l], lam=lru_lambda[l], w_out=w_out_b, w_ff1=w_ff1_b,
                   w_ff2=w_ff2_b) for l in range(depth)]
    y_prompt = _trunk(x_prompt, mod, 0, layers, g_final)
    y_sample = _trunk(x_sample, mod, n_prompt, layers, g_final)
    return (y_prompt, y_sample)
```

```python
import functools
import math

import numpy as np
import jax
import jax.numpy as jnp
from jax import lax
from jax.experimental import pallas as pl
from jax.experimental.pallas import tpu as pltpu

F32 = jnp.float32
BF16 = jnp.bfloat16

D_MODEL = 1024
HEAD_DIM = 64
D_FOURIER = 512
D_LRU = 512
N_HEADS = 8
D_FF = 4096
N_MOD = 6
EPS = 1e-6
LRU_C = 8.0
CONV_WIDTH = 4

LANES = 128
SUBLANES = 8
BF16_ROWS = 16
N_SLAB = D_LRU // LANES
ROW_TILE = 512
IN_TILE = 1024
SEG = ROW_TILE // SUBLANES
SEG_PITCH = SEG + SUBLANES
TILE_PITCH = SUBLANES * SEG_PITCH
LRU_TILES_PER_STEP = 4
N_STATE = 3
LOG2E = 1.4426950408889634
COL_G, COL_R, COL_A, COL_V = 0, D_LRU, 2 * D_LRU, 2 * D_LRU + D_FOURIER
N_STATS = 4
N_DIR = 2
MXU_K = 256
GATE_COLS = 4 * MXU_K
DFT_N2 = 256
DFT_GROUP = BF16_ROWS
DFT_ROWS_PER_STEP = 2048
CARRY_UNROLL = 4
FF_CHUNK = 1024
MOD_COLS = 1536
VMEM_LIMIT = 56 * 1024 * 1024


def _const_spec(shape):
    n = len(shape)
    return pl.BlockSpec(shape, lambda *_: (0,) * n, pipeline_mode=pl.Buffered(1))


def _layer_spec(layer, shape):
    n = len(shape)
    return pl.BlockSpec((None,) + tuple(shape), lambda *_: (layer,) + (0,) * n, pipeline_mode=pl.Buffered(1))


def _dot3(a, b):
    a_hi = a.astype(BF16)
    b_hi = b.astype(BF16)
    a_lo = (a - a_hi.astype(F32)).astype(BF16)
    b_lo = (b - b_hi.astype(F32)).astype(BF16)
    return (jnp.dot(a_hi, b_hi, preferred_element_type=F32) + jnp.dot(a_hi, b_lo, preferred_element_type=F32)
            + jnp.dot(a_lo, b_hi, preferred_element_type=F32))


def _params(*sem):
    return pltpu.CompilerParams(dimension_semantics=sem, vmem_limit_bytes=VMEM_LIMIT)


def _mod_kernel(c_ref, w_ref, b_ref, o_ref):
    c = c_ref[...]
    s = c * jax.nn.sigmoid(c)
    o_ref[...] = _dot3(s, w_ref[...]) + b_ref[...]


def _modulation(c_all, w_mod, b_mod):
    depth = w_mod.shape[0]
    n_out = N_MOD * D_MODEL
    rows = c_all.shape[0]
    return pl.pallas_call(
        _mod_kernel,
        grid=(depth, n_out // MOD_COLS),
        in_specs=[
            pl.BlockSpec((rows, D_MODEL), lambda l, j: (0, 0)),
            pl.BlockSpec((None, D_MODEL, MOD_COLS), lambda l, j: (l, 0, j)),
            pl.BlockSpec((None, 1, MOD_COLS), lambda l, j: (l, 0, j)),
        ],
        out_specs=pl.BlockSpec((None, rows, MOD_COLS), lambda l, j: (l, 0, j)),
        out_shape=jax.ShapeDtypeStruct((depth, rows, n_out), F32),
        compiler_params=_params("parallel", "parallel"),
        name="adaln_mod",
    )(c_all, w_mod, b_mod.reshape(depth, 1, n_out))


def _prep_kernel(win_ref, bdwf_ref, bdc_ref, bds_ref, o_ref):
    for blk in range(D_FOURIER // MXU_K):
        sl = slice(blk * MXU_K, (blk + 1) * MXU_K)
        wf = bdwf_ref[sl, sl]
        ma = _dot3(bdc_ref[sl, sl], wf)
        mv = -_dot3(bds_ref[sl, sl], wf)
        wi = win_ref[:, sl]
        o_ref[:, COL_A + blk * MXU_K:COL_A + (blk + 1) * MXU_K] = _dot3(wi, ma).astype(BF16)
        o_ref[:, COL_V + blk * MXU_K:COL_V + (blk + 1) * MXU_K] = _dot3(wi, mv).astype(BF16)
    o_ref[:, COL_R:COL_R + D_LRU] = win_ref[:, D_FOURIER:D_FOURIER + D_LRU].astype(BF16)
    o_ref[:, COL_G:COL_G + D_LRU] = win_ref[:, D_FOURIER + D_LRU:].astype(BF16)


def _block_diag(w):
    h, d, e = w.shape[-3:]
    eye = jnp.eye(h, dtype=w.dtype)
    out = w[..., :, :, None, :] * eye[:, None, :, None]
    return out.reshape(w.shape[:-3] + (h * d, h * e))


def _channel_dft_tables():
    k = np.arange(HEAD_DIM)
    ang = 2.0 * np.pi * np.outer(k, k) / HEAD_DIM
    eye = np.eye(N_HEADS)
    return (jnp.asarray(np.kron(eye, np.cos(ang)), F32), jnp.asarray(np.kron(eye, np.sin(ang)), F32))


def _fold_in_weights(w_in, w_fourier):
    depth = w_in.shape[0]
    d_in = w_in.shape[2]
    n_out = d_in + D_FOURIER
    bdc, bds = _channel_dft_tables()
    bdwf = _block_diag(w_fourier)
    return pl.pallas_call(
        _prep_kernel,
        grid=(depth,),
        in_specs=[
            pl.BlockSpec((None, D_MODEL, d_in), lambda l: (l, 0, 0)),
            pl.BlockSpec((None, D_FOURIER, D_FOURIER), lambda l: (l, 0, 0)),
            _const_spec((D_FOURIER, D_FOURIER)),
            _const_spec((D_FOURIER, D_FOURIER)),
        ],
        out_specs=pl.BlockSpec((None, D_MODEL, n_out), lambda l: (l, 0, 0)),
        out_shape=jax.ShapeDtypeStruct((depth, D_MODEL, n_out), BF16),
        compiler_params=_params("parallel"),
        name="fold_in_weights",
    )(w_in, bdwf, bdc, bds)


def _rms_norm(x, g):
    ms = jnp.mean(x * x, axis=-1, keepdims=True)
    return x * lax.rsqrt(ms + EPS) * g


def _gelu_tanh(x):
    c = math.sqrt(2.0 / math.pi)
    return 0.5 * x * (1.0 + jnp.tanh(c * (x + 0.044715 * (x * x * x))))


def _mod_spec(layer, row0):
    return pl.BlockSpec((None, None, 1, N_MOD * D_MODEL), lambda b, i: (layer, row0 + b, 0, 0))


def _slab_shape(b, s):
    return jax.ShapeDtypeStruct((b, N_SLAB, s // ROW_TILE * TILE_PITCH, LANES), F32)


def _inproj_kernel(x_ref, mod_ref, g_ref, w_ref, ua_ref, uv_ref, ur_ref, gg_ref, *, n1):
    shift = mod_ref[:, 0:D_MODEL]
    scale = mod_ref[:, D_MODEL:2 * D_MODEL]
    h = _rms_norm(x_ref[...], g_ref[...] * (1.0 + scale)) + shift
    z = jnp.dot(h.astype(BF16), w_ref[...], preferred_element_type=F32)
    n_grp = n1 // SUBLANES
    for c in range(N_SLAB):
        for s2 in range(IN_TILE // n1):
            for grp in range(n_grp):
                src = slice(s2 * n1 + grp * SUBLANES, s2 * n1 + (grp + 1) * SUBLANES)
                dst = slice(s2 * SUBLANES, (s2 + 1) * SUBLANES)
                ua_ref[c, grp, dst, :] = z[src, COL_A + c * LANES:COL_A + (c + 1) * LANES]
                uv_ref[c, grp, dst, :] = z[src, COL_V + c * LANES:COL_V + (c + 1) * LANES]
    pad = jnp.zeros((SEG_PITCH - SEG, LANES), F32)
    for c in range(N_SLAB):
        ur = z[:, COL_R + c * LANES:COL_R + (c + 1) * LANES]
        for j in range(IN_TILE // SEG):
            ur_ref[c, j * SEG_PITCH:j * SEG_PITCH + SEG, :] = ur[j * SEG:(j + 1) * SEG]
            ur_ref[c, j * SEG_PITCH + SEG:(j + 1) * SEG_PITCH, :] = pad
    gg_ref[...] = _gelu_tanh(z[:, COL_G:COL_G + D_LRU]).astype(BF16)


def _in_projection(x, mod, layer, row0, g_mix, w_fold):
    b, s, _ = x.shape
    n_out = w_fold.shape[-1]
    slab = _slab_shape(b, s)
    slab_spec = pl.BlockSpec((None, N_SLAB, IN_TILE // SEG * SEG_PITCH, LANES), lambda bi, i: (bi, 0, i, 0))
    n1 = s // DFT_N2
    n_grp = n1 // SUBLANES
    grp_rows = IN_TILE // n1 * SUBLANES
    half = jax.ShapeDtypeStruct((b, N_SLAB, n_grp, DFT_N2 * SUBLANES, LANES), F32)
    half_spec = pl.BlockSpec((None, N_SLAB, n_grp, grp_rows, LANES), lambda bi, i: (bi, 0, 0, i, 0))
    return pl.pallas_call(
        functools.partial(_inproj_kernel, n1=n1),
        grid=(b, s // IN_TILE),
        in_specs=[
            pl.BlockSpec((None, IN_TILE, D_MODEL), lambda bi, i: (bi, i, 0)),
            _mod_spec(layer, row0),
            _const_spec((1, D_MODEL)),
            _layer_spec(layer, (D_MODEL, n_out)),
        ],
        out_specs=[half_spec, half_spec, slab_spec,
                   pl.BlockSpec((None, IN_TILE, D_LRU), lambda bi, i: (bi, i, 0))],
        out_shape=[half, half, slab, jax.ShapeDtypeStruct((b, s, D_LRU), BF16)],
        compiler_params=_params("parallel", "parallel"),
        name="inproj",
    )(x, mod, g_mix.reshape(1, D_MODEL), w_fold)


def _dft_tables(seq):
    n2 = DFT_N2
    n1 = seq // n2
    k2 = np.arange(n2)
    ang = 2.0 * np.pi * np.outer(k2, k2) / n2
    c2 = np.cos(ang) / math.sqrt(n2)
    s2 = np.sin(ang) / math.sqrt(n2)
    fa = np.concatenate([c2, -s2], axis=0)
    fv = np.concatenate([s2, c2], axis=0)
    g = DFT_GROUP
    n_groups = n2 // g
    rows = n1 * g
    k1 = np.arange(n1)
    s1 = np.arange(n1)
    kp = np.arange(g)
    scale = 1.0 / math.sqrt(n1 * HEAD_DIM)
    lc = np.zeros((n_groups, n1, g, n1, g), np.float64)
    ls = np.zeros((n_groups, n1, g, n1, g), np.float64)
    for grp in range(n_groups):
        k = n2 * k1[:, None] + (g * grp + kp)[None, :]
        ang_b = 2.0 * np.pi * (k[:, :, None] * s1[None, None, :] % seq) / seq
        for q in range(g):
            lc[grp, :, q, :, q] = np.cos(ang_b[:, q, :]) * scale
            ls[grp, :, q, :, q] = np.sin(ang_b[:, q, :]) * scale
    lc = lc.reshape(n_groups, rows, rows)
    ls = ls.reshape(n_groups, rows, rows)
    as_bf16 = lambda a: jnp.asarray(a.astype(np.float32)).astype(BF16)
    return as_bf16(fa), as_bf16(fv), as_bf16(lc), as_bf16(ls)


def _dft_kernel(a_ref, v_ref, fa_ref, fv_ref, lc_ref, ls_ref, y_ref, tre_scr, tim_scr, *, n1, groups_per_step):
    k = pl.program_id(1)
    n_grp = n1 // SUBLANES

    @pl.when(k < n_grp)
    def _():
        for q in range(SUBLANES):
            rows = pl.ds(q, DFT_N2, stride=SUBLANES)
            a = jnp.concatenate([a_ref[c, rows, :] for c in range(N_SLAB)], axis=1).astype(BF16)
            v = jnp.concatenate([v_ref[c, rows, :] for c in range(N_SLAB)], axis=1).astype(BF16)
            o = (jnp.dot(fa_ref[...], a, preferred_element_type=F32)
                 + jnp.dot(fv_ref[...], v, preferred_element_type=F32))
            tre_scr[k * SUBLANES + q] = o[0:DFT_N2].astype(BF16)
            tim_scr[k * SUBLANES + q] = o[DFT_N2:].astype(BF16)

    @pl.when(k >= n_grp)
    def _():
        rows = n1 * DFT_GROUP
        for j in range(groups_per_step):
            first = pl.multiple_of(((k - n_grp) * groups_per_step + j) * DFT_GROUP, DFT_GROUP)
            tr = tre_scr[:, pl.ds(first, DFT_GROUP), :].reshape(rows, D_FOURIER)
            ti = tim_scr[:, pl.ds(first, DFT_GROUP), :].reshape(rows, D_FOURIER)
            o = (jnp.dot(lc_ref[j], tr, preferred_element_type=F32)
                 + jnp.dot(ls_ref[j], ti, preferred_element_type=F32))
            y_ref[:, j * DFT_GROUP:(j + 1) * DFT_GROUP, :] = o.astype(BF16).reshape(n1, DFT_GROUP, D_FOURIER)


def _fourier_mix(ua, uv, s):
    b = ua.shape[0]
    n2 = DFT_N2
    n1 = s // n2
    n_grp = n1 // SUBLANES
    rows = n1 * DFT_GROUP
    groups_per_step = DFT_ROWS_PER_STEP // rows
    n_out = n2 // (DFT_GROUP * groups_per_step)
    fa, fv, lc, ls = _dft_tables(s)
    in_spec = pl.BlockSpec((None, N_SLAB, None, n2 * SUBLANES, LANES),
                           lambda bi, k: (bi, 0, jnp.minimum(k, n_grp - 1), 0, 0))
    l_spec = pl.BlockSpec((groups_per_step, rows, rows), lambda bi, k: (jnp.maximum(k - n_grp, 0), 0, 0))
    t_scr = pltpu.VMEM((n1, n2, D_FOURIER), BF16)
    y = pl.pallas_call(
        functools.partial(_dft_kernel, n1=n1, groups_per_step=groups_per_step),
        grid=(b, n_grp + n_out),
        in_specs=[in_spec, in_spec, _const_spec((2 * n2, n2)), _const_spec((2 * n2, n2)), l_spec, l_spec],
        out_specs=pl.BlockSpec((None, n1, DFT_GROUP * groups_per_step, D_FOURIER),
                               lambda bi, k: (bi, 0, jnp.maximum(k - n_grp, 0), 0)),
        out_shape=jax.ShapeDtypeStruct((b, n1, n2, D_FOURIER), BF16),
        scratch_shapes=[t_scr, t_scr],
        compiler_params=_params("parallel", "arbitrary"),
        name="seq_dft",
    )(ua, uv, fa, fv, lc, ls)
    return y.reshape(b, s, D_FOURIER)


def _lru_kernel(ur_ref, prev_ref, next_ref, cw_ref, cb_ref, wg_ref, bg_ref, lam_ref,
                st_ref, ah_ref, uc_scr, pre_scr, hf_scr):
    i = pl.program_id(1)
    n_steps = pl.num_programs(1)
    sub = lax.broadcasted_iota(jnp.int32, (SUBLANES, LANES), 0)
    slab_lanes = [slice(c * LANES, (c + 1) * LANES) for c in range(N_SLAB)]

    cw = [[jnp.broadcast_to(0.5 * cw_ref[k:k + 1, slab_lanes[c]], (SUBLANES, LANES)) for k in range(CONV_WIDTH)]
          for c in range(N_SLAB)]
    cbias = [jnp.broadcast_to(0.5 * cb_ref[:, slab_lanes[c]], (SUBLANES, LANES)) for c in range(N_SLAB)]
    neg_lam = -lam_ref[...]
    softplus = jnp.maximum(neg_lam, 0.0) + jnp.log1p(jnp.exp(-jnp.abs(neg_lam)))
    k_all = (-0.5 * LRU_C * LOG2E) * softplus
    k_rows = [[jnp.broadcast_to(k_all[d:d + 1, slab_lanes[c]], (SUBLANES, LANES)) for c in range(N_SLAB)]
              for d in range(2)]
    pad = jnp.zeros((SEG_PITCH - SEG, LANES), F32)

    def scan_inputs(rows, c, d):
        col = (c * LANES // MXU_K) * GATE_COLS + d * 2 * MXU_K + (c * LANES) % MXU_K
        th_r = jnp.tanh(pre_scr[rows, col:col + LANES])
        th_i = jnp.tanh(pre_scr[rows, col + MXU_K:col + MXU_K + LANES])
        k = k_rows[d][c]
        a = jnp.exp2(k * th_r + k)
        m = 1.0 - a * a
        mult = jnp.where(m > 0.0, m * lax.rsqrt(m), 0.0)
        return a, mult * ((th_i + 1.0) * uc_scr[rows, slab_lanes[c]])

    for u in range(LRU_TILES_PER_STEP):
        base = u * TILE_PITCH
        last_seg_end = base - SEG_PITCH + SEG

        def segment_rows(ref, c, t, base=base):
            return ref[c, pl.ds(base + t, SUBLANES, stride=SEG_PITCH), :]

        for c in range(N_SLAB):
            for j in range(SUBLANES):
                pad_rows = slice(base + j * SEG_PITCH + SEG, base + (j + 1) * SEG_PITCH)
                for plane in range(N_STATE):
                    st_ref[plane, c, pad_rows, :] = pad

        for c in range(N_SLAB):
            lanes = slab_lanes[c]
            if u == 0:
                prev_rows = [jnp.where(i > 0, prev_ref[c, SUBLANES - 2 + q:SUBLANES - 1 + q, :], 0.0) for q in range(2)]
            else:
                prev_rows = [ur_ref[c, last_seg_end - 2 + q:last_seg_end - 1 + q, :] for q in range(2)]
            if u == LRU_TILES_PER_STEP - 1:
                next_row = jnp.where(i < n_steps - 1, next_ref[c, 0:1, :], 0.0)
            else:
                next_row = ur_ref[c, base + TILE_PITCH:base + TILE_PITCH + 1, :]
            first = segment_rows(ur_ref, c, 0)
            window = [jnp.where(sub == 0, prev_rows[q], pltpu.roll(segment_rows(ur_ref, c, SEG - 2 + q), 1, 0))
                      for q in range(2)] + [first]
            for t in range(SEG):
                if t + 1 < SEG:
                    nxt = segment_rows(ur_ref, c, t + 1)
                else:
                    nxt = jnp.where(sub == SUBLANES - 1, next_row, pltpu.roll(first, SUBLANES - 1, 0))
                window.append(nxt)
                uc = (cbias[c] + window[0] * cw[c][0] + window[1] * cw[c][1] + window[2] * cw[c][2]
                      + window[3] * cw[c][3])
                uc_scr[t * SUBLANES:(t + 1) * SUBLANES, lanes] = uc
                window.pop(0)

        for hh in range(D_LRU // MXU_K):
            cols = slice(hh * GATE_COLS, (hh + 1) * GATE_COLS)
            lhs = uc_scr[:, hh * MXU_K:(hh + 1) * MXU_K].astype(BF16)
            pre_scr[:, cols] = jnp.dot(lhs, wg_ref[hh], preferred_element_type=F32) + bg_ref[:, cols]

        segs = slice(u * SUBLANES, (u + 1) * SUBLANES)
        hs = [jnp.zeros((SUBLANES, LANES), F32) for _ in range(N_SLAB)]
        ps = [jnp.ones((SUBLANES, LANES), F32) for _ in range(N_SLAB)]
        for t in range(SEG):
            rows = slice(t * SUBLANES, (t + 1) * SUBLANES)
            for c in range(N_SLAB):
                a, b = scan_inputs(rows, c, 0)
                hs[c] = a * hs[c] + b
                ps[c] = a * ps[c]
                hf_scr[rows, slab_lanes[c]] = hs[c]
                st_ref[1, c, pl.ds(base + t, SUBLANES, stride=SEG_PITCH), :] = ps[c]
        for c in range(N_SLAB):
            ah_ref[0, segs, slab_lanes[c]] = ps[c]
            ah_ref[1, segs, slab_lanes[c]] = hs[c]

        hs = [jnp.zeros((SUBLANES, LANES), F32) for _ in range(N_SLAB)]
        ps = [jnp.ones((SUBLANES, LANES), F32) for _ in range(N_SLAB)]
        for t in reversed(range(SEG)):
            rows = slice(t * SUBLANES, (t + 1) * SUBLANES)
            for c in range(N_SLAB):
                a, b = scan_inputs(rows, c, 1)
                hs[c] = a * hs[c] + b
                ps[c] = a * ps[c]
                st_ref[0, c, pl.ds(base + t, SUBLANES, stride=SEG_PITCH), :] = hf_scr[rows, slab_lanes[c]] + hs[c]
                st_ref[2, c, pl.ds(base + t, SUBLANES, stride=SEG_PITCH), :] = ps[c]
        for c in range(N_SLAB):
            ah_ref[2, segs, slab_lanes[c]] = ps[c]
            ah_ref[3, segs, slab_lanes[c]] = hs[c]


def _lru_local(ur, layer, conv_w, conv_b, wg, bg, lam):
    b, _, rows, _ = ur.shape
    step_rows = LRU_TILES_PER_STEP * TILE_PITCH
    n_steps = rows // step_rows
    blocks_per_step = step_rows // SUBLANES
    last_block = rows // SUBLANES - 1
    s = rows // TILE_PITCH * ROW_TILE
    slab = _slab_shape(b, s)
    slab_spec = pl.BlockSpec((None, N_SLAB, step_rows, LANES), lambda bi, i: (bi, 0, i, 0))
    prev_spec = pl.BlockSpec((None, N_SLAB, SUBLANES, LANES),
                             lambda bi, i: (bi, 0, jnp.maximum(i * blocks_per_step - 2, 0), 0))
    next_spec = pl.BlockSpec((None, N_SLAB, SUBLANES, LANES),
                             lambda bi, i: (bi, 0, jnp.minimum((i + 1) * blocks_per_step, last_block), 0))
    n_seg = s // SEG
    n_half = D_LRU // MXU_K
    tile_f32 = pltpu.VMEM((ROW_TILE, D_LRU), F32)
    return pl.pallas_call(
        _lru_kernel,
        grid=(b, n_steps),
        in_specs=[
            slab_spec, prev_spec, next_spec,
            _const_spec((CONV_WIDTH, D_LRU)),
            _const_spec((1, D_LRU)),
            _layer_spec(layer, (n_half, MXU_K, GATE_COLS)),
            _const_spec((1, n_half * GATE_COLS)),
            _const_spec((2, D_LRU)),
        ],
        out_specs=[pl.BlockSpec((None, N_STATE, N_SLAB, step_rows, LANES), lambda bi, i: (bi, 0, 0, i, 0)),
                   pl.BlockSpec((None, N_STATS, LRU_TILES_PER_STEP * SUBLANES, D_LRU),
                                lambda bi, i: (bi, 0, i, 0))],
        out_shape=[jax.ShapeDtypeStruct((b, N_STATE) + slab.shape[1:], F32),
                   jax.ShapeDtypeStruct((b, N_STATS, n_seg, D_LRU), F32)],
        scratch_shapes=[tile_f32, pltpu.VMEM((ROW_TILE, n_half * GATE_COLS), F32), tile_f32],
        compiler_params=_params("parallel", "parallel"),
        name="lru_local",
    )(ur, ur, ur, conv_w, conv_b.reshape(1, D_LRU), wg, bg, lam)


def _carry_kernel(ah_ref, cc_ref, *, n_seg):
    def step(n, carry):
        cf, cb = carry
        sf = n
        sb = n_seg - 1 - n
        cc_ref[0, pl.ds(sf, 1), :] = cf
        cc_ref[1, pl.ds(sb, 1), :] = cb
        return (ah_ref[1, pl.ds(sf, 1), :] + ah_ref[0, pl.ds(sf, 1), :] * cf,
                ah_ref[3, pl.ds(sb, 1), :] + ah_ref[2, pl.ds(sb, 1), :] * cb)

    zero = jnp.zeros((1, D_LRU), F32)
    lax.fori_loop(0, n_seg, step, (zero, zero), unroll=CARRY_UNROLL)


def _segment_carries(ah):
    b, _, n_seg, _ = ah.shape
    c_shape = jax.ShapeDtypeStruct((b, N_DIR, n_seg, D_LRU), F32)
    c_spec = pl.BlockSpec((None, N_DIR, n_seg, D_LRU), lambda bi: (bi, 0, 0, 0))
    return pl.pallas_call(
        functools.partial(_carry_kernel, n_seg=n_seg),
        grid=(b,),
        in_specs=[pl.BlockSpec((None, N_STATS, n_seg, D_LRU), lambda bi: (bi, 0, 0, 0))],
        out_specs=c_spec,
        out_shape=c_shape,
        compiler_params=_params("parallel"),
        name="lru_carry",
    )(ah)


def _ffn_kernel(x_ref, yf_ref, gg_ref, st_ref, cc_ref, mod_ref, g_ref,
                wo_ref, w1_ref, w2_ref, gfin_ref, o_ref, yr_scr, *, final_norm):
    for c in range(N_SLAB):
        lanes = slice(c * LANES, (c + 1) * LANES)
        for j in range(ROW_TILE // SEG):
            rows = slice(j * SEG_PITCH, j * SEG_PITCH + SEG)
            hsum = (st_ref[0, c, rows, :] + st_ref[1, c, rows, :] * cc_ref[0, j:j + 1, lanes]
                    + st_ref[2, c, rows, :] * cc_ref[1, j:j + 1, lanes])
            gate = gg_ref[j * SEG:(j + 1) * SEG, lanes].astype(F32)
            yr_scr[j * SEG:(j + 1) * SEG, lanes] = (hsum * gate).astype(BF16)
    gate1 = mod_ref[:, 2 * D_MODEL:3 * D_MODEL]
    shift2 = mod_ref[:, 3 * D_MODEL:4 * D_MODEL]
    scale2 = mod_ref[:, 4 * D_MODEL:5 * D_MODEL]
    gate2 = mod_ref[:, 5 * D_MODEL:6 * D_MODEL]
    mix = (jnp.dot(yf_ref[...], wo_ref[0:D_FOURIER, :], preferred_element_type=F32)
           + jnp.dot(yr_scr[...], wo_ref[D_FOURIER:, :], preferred_element_type=F32))
    x1 = x_ref[...] + gate1 * mix
    h = (_rms_norm(x1, g_ref[...] * (1.0 + scale2)) + shift2).astype(BF16)
    acc = jnp.zeros((ROW_TILE, D_MODEL), F32)
    for k in range(D_FF // FF_CHUNK):
        cols = slice(k * FF_CHUNK, (k + 1) * FF_CHUNK)
        t = jnp.maximum(jnp.dot(h, w1_ref[:, cols], preferred_element_type=F32), 0.0)
        acc = acc + jnp.dot((t * t).astype(BF16), w2_ref[cols, :], preferred_element_type=F32)
    x2 = x1 + gate2 * acc
    if final_norm:
        x2 = _rms_norm(x2, gfin_ref[...])
    o_ref[...] = x2


def _out_and_mlp(x, yf, gg, st, cc, mod, layer, row0, g_mlp, w_out, w_ff1, w_ff2, g_final,
                 final_norm):
    b, s, _ = x.shape
    segs = ROW_TILE // SEG
    x_spec = pl.BlockSpec((None, ROW_TILE, D_MODEL), lambda bi, i: (bi, i, 0))
    st_spec = pl.BlockSpec((None, N_STATE, N_SLAB, TILE_PITCH, LANES), lambda bi, i: (bi, 0, 0, i, 0))
    c_spec = pl.BlockSpec((None, N_DIR, segs, D_LRU), lambda bi, i: (bi, 0, i, 0))
    return pl.pallas_call(
        functools.partial(_ffn_kernel, final_norm=final_norm),
        grid=(b, s // ROW_TILE),
        in_specs=[
            x_spec,
            pl.BlockSpec((None, ROW_TILE, D_FOURIER), lambda bi, i: (bi, i, 0)),
            pl.BlockSpec((None, ROW_TILE, D_LRU), lambda bi, i: (bi, i, 0)),
            st_spec, c_spec,
            _mod_spec(layer, row0),
            _const_spec((1, D_MODEL)),
            _layer_spec(layer, (D_MODEL, D_MODEL)),
            _layer_spec(layer, (D_MODEL, D_FF)),
            _layer_spec(layer, (D_FF, D_MODEL)),
            _const_spec((1, D_MODEL)),
        ],
        out_specs=x_spec,
        out_shape=jax.ShapeDtypeStruct(x.shape, F32),
        scratch_shapes=[pltpu.VMEM((ROW_TILE, D_LRU), BF16)],
        compiler_params=_params("parallel", "parallel"),
        name="outproj_mlp",
    )(x, yf, gg, st, cc, mod, g_mlp.reshape(1, D_MODEL), w_out, w_ff1, w_ff2,
      g_final.reshape(1, D_MODEL))


def _gate_weights(w_rgate, b_rgate, w_igate, b_igate):
    depth = w_rgate.shape[0]
    n_half = D_LRU // MXU_K
    heads_per_half = N_HEADS // n_half
    w_halves, b_halves = [], []
    for hh in range(n_half):
        hs = slice(hh * heads_per_half, (hh + 1) * heads_per_half)
        w_cols, b_cols = [], []
        for d in range(2):
            for w, bias in ((w_rgate, b_rgate), (w_igate, b_igate)):
                w_cols.append(_block_diag(w[:, d, hs]))
                b_cols.append(bias[:, d, hs].reshape(depth, MXU_K))
        w_halves.append(jnp.concatenate(w_cols, axis=-1))
        b_halves.append(jnp.concatenate(b_cols, axis=-1))
    wg = jnp.stack(w_halves, axis=1).astype(BF16)
    bg = (0.5 * jnp.concatenate(b_halves, axis=-1)).reshape(depth, 1, n_half * GATE_COLS)
    return wg, bg


def _trunk(x, mod, row0, layers, g_final):
    depth = len(layers)
    for l, p in enumerate(layers):
        ua, uv, ur, gg = _in_projection(x, mod, l, row0, p["g_mix"], p["w_fold"])
        yf = _fourier_mix(ua, uv, x.shape[1])
        st, ah = _lru_local(ur, l, p["conv_w"], p["conv_b"], p["wg"], p["bg"], p["lam"])
        cc = _segment_carries(ah)
        x = _out_and_mlp(x, yf, gg, st, cc, mod, l, row0, p["g_mlp"], p["w_out"],
                         p["w_ff1"], p["w_ff2"], g_final, final_norm=(l == depth - 1))
    return x


def kernel(x_prompt, x_sample, c_prompt, c_sample, g_mix, g_mlp, w_mod, b_mod, w_in, w_fourier,
           conv_w, conv_b, w_rgate, b_rgate, w_igate, b_igate, lru_lambda, w_out, w_ff1, w_ff2,
           g_final):
    depth = w_in.shape[0]
    n_prompt = c_prompt.shape[0]
    n_cond = n_prompt + c_sample.shape[0]
    pad = -n_cond % SUBLANES
    c_all = jnp.concatenate([c_prompt, c_sample, jnp.zeros((pad, D_MODEL), F32)], axis=0)
    mod = _modulation(c_all, w_mod, b_mod)
    mod = mod.reshape(depth, n_cond + pad, 1, N_MOD * D_MODEL)

    w_fold = _fold_in_weights(w_in, w_fourier)
    wg, bg = _gate_weights(w_rgate, b_rgate, w_igate, b_igate)
    w_out_b = w_out.astype(BF16)
    w_ff1_b = w_ff1.astype(BF16)
    w_ff2_b = w_ff2.astype(BF16)
    layers = [dict(g_mix=g_mix[l], g_mlp=g_mlp[l], w_fold=w_fold, conv_w=conv_w[l], conv_b=conv_b[l],
                   wg=wg, bg=bg[l], lam=lru_lambda[l], w_out=w_out_b, w_ff1=w_ff1_b,
                   w_ff2=w_ff2_b) for l in range(depth)]
    y_prompt = _trunk(x_prompt, mod, 0, layers, g_final)
    y_sample = _trunk(x_sample, mod, n_prompt, layers, g_final)
    return (y_prompt, y_sample)
```

```python
import functools
import math

import numpy as np
import jax
import jax.numpy as jnp
from jax import lax
from jax.experimental import pallas as pl
from jax.experimental.pallas import tpu as pltpu

F32 = jnp.float32
BF16 = jnp.bfloat16

D_MODEL = 1024
HEAD_DIM = 64
D_FOURIER = 512
D_LRU = 512
N_HEADS = 8
D_FF = 4096
N_MOD = 6
EPS = 1e-6
LRU_C = 8.0
CONV_WIDTH = 4

LANES = 128
SUBLANES = 8
BF16_ROWS = 16
N_SLAB = D_LRU // LANES
ROW_TILE = 512
IN_TILE = 1024
SEG = ROW_TILE // SUBLANES
SEG_PITCH = SEG + SUBLANES
TILE_PITCH = SUBLANES * SEG_PITCH
LRU_TILES_PER_STEP = 4
N_STATE = 3
LOG2E = 1.4426950408889634
COL_G, COL_R, COL_A, COL_V = 0, D_LRU, 2 * D_LRU, 2 * D_LRU + D_FOURIER
N_STATS = 4
N_DIR = 2
MXU_K = 256
GATE_COLS = 4 * MXU_K
DFT_N2 = 256
DFT_GROUP = BF16_ROWS
DFT_ROWS_PER_STEP = 2048
CARRY_UNROLL = 4
FF_CHUNK = 1024
MOD_COLS = 1536
VMEM_LIMIT = 56 * 1024 * 1024


def _const_spec(shape):
    n = len(shape)
    return pl.BlockSpec(shape, lambda *_: (0,) * n, pipeline_mode=pl.Buffered(1))


def _layer_spec(layer, shape):
    n = len(shape)
    return pl.BlockSpec((None,) + tuple(shape), lambda *_: (layer,) + (0,) * n, pipeline_mode=pl.Buffered(1))


def _dot3(a, b):
    a_hi = a.astype(BF16)
    b_hi = b.astype(BF16)
    a_lo = (a - a_hi.astype(F32)).astype(BF16)
    b_lo = (b - b_hi.astype(F32)).astype(BF16)
    return (jnp.dot(a_hi, b_hi, preferred_element_type=F32) + jnp.dot(a_hi, b_lo, preferred_element_type=F32)
            + jnp.dot(a_lo, b_hi, preferred_element_type=F32))


def _params(*sem):
    return pltpu.CompilerParams(dimension_semantics=sem, vmem_limit_bytes=VMEM_LIMIT)


def _mod_kernel(c_ref, w_ref, b_ref, o_ref):
    c = c_ref[...]
    s = c * jax.nn.sigmoid(c)
    o_ref[...] = _dot3(s, w_ref[...]) + b_ref[...]


def _modulation(c_all, w_mod, b_mod):
    depth = w_mod.shape[0]
    n_out = N_MOD * D_MODEL
    rows = c_all.shape[0]
    return pl.pallas_call(
        _mod_kernel,
        grid=(depth, n_out // MOD_COLS),
        in_specs=[
            pl.BlockSpec((rows, D_MODEL), lambda l, j: (0, 0)),
            pl.BlockSpec((None, D_MODEL, MOD_COLS), lambda l, j: (l, 0, j)),
            pl.BlockSpec((None, 1, MOD_COLS), lambda l, j: (l, 0, j)),
        ],
        out_specs=pl.BlockSpec((None, rows, MOD_COLS), lambda l, j: (l, 0, j)),
        out_shape=jax.ShapeDtypeStruct((depth, rows, n_out), F32),
        compiler_params=_params("parallel", "parallel"),
        name="adaln_mod",
    )(c_all, w_mod, b_mod.reshape(depth, 1, n_out))


def _prep_kernel(win_ref, bdwf_ref, bdc_ref, bds_ref, o_ref):
    for blk in range(D_FOURIER // MXU_K):
        sl = slice(blk * MXU_K, (blk + 1) * MXU_K)
        wf = bdwf_ref[sl, sl]
        ma = _dot3(bdc_ref[sl, sl], wf)
        mv = -_dot3(bds_ref[sl, sl], wf)
        wi = win_ref[:, sl]
        o_ref[:, COL_A + blk * MXU_K:COL_A + (blk + 1) * MXU_K] = _dot3(wi, ma).astype(BF16)
        o_ref[:, COL_V + blk * MXU_K:COL_V + (blk + 1) * MXU_K] = _dot3(wi, mv).astype(BF16)
    o_ref[:, COL_R:COL_R + D_LRU] = win_ref[:, D_FOURIER:D_FOURIER + D_LRU].astype(BF16)
    o_ref[:, COL_G:COL_G + D_LRU] = win_ref[:, D_FOURIER + D_LRU:].astype(BF16)


def _block_diag(w):
    h, d, e = w.shape[-3:]
    eye = jnp.eye(h, dtype=w.dtype)
    out = w[..., :, :, None, :] * eye[:, None, :, None]
    return out.reshape(w.shape[:-3] + (h * d, h * e))


def _channel_dft_tables():
    k = np.arange(HEAD_DIM)
    ang = 2.0 * np.pi * np.outer(k, k) / HEAD_DIM
    eye = np.eye(N_HEADS)
    return (jnp.asarray(np.kron(eye, np.cos(ang)), F32), jnp.asarray(np.kron(eye, np.sin(ang)), F32))


def _fold_in_weights(w_in, w_fourier):
    depth = w_in.shape[0]
    d_in = w_in.shape[2]
    n_out = d_in + D_FOURIER
    bdc, bds = _channel_dft_tables()
    bdwf = _block_diag(w_fourier)
    return pl.pallas_call(
        _prep_kernel,
        grid=(depth,),
        in_specs=[
            pl.BlockSpec((None, D_MODEL, d_in), lambda l: (l, 0, 0)),
            pl.BlockSpec((None, D_FOURIER, D_FOURIER), lambda l: (l, 0, 0)),
            _const_spec((D_FOURIER, D_FOURIER)),
            _const_spec((D_FOURIER, D_FOURIER)),
        ],
        out_specs=pl.BlockSpec((None, D_MODEL, n_out), lambda l: (l, 0, 0)),
        out_shape=jax.ShapeDtypeStruct((depth, D_MODEL, n_out), BF16),
        compiler_params=_params("parallel"),
        name="fold_in_weights",
    )(w_in, bdwf, bdc, bds)


def _rms_norm(x, g):
    ms = jnp.mean(x * x, axis=-1, keepdims=True)
    return x * lax.rsqrt(ms + EPS) * g


def _gelu_tanh(x):
    c = math.sqrt(2.0 / math.pi)
    return 0.5 * x * (1.0 + jnp.tanh(c * (x + 0.044715 * (x * x * x))))


def _mod_spec(layer, row0):
    return pl.BlockSpec((None, None, 1, N_MOD * D_MODEL), lambda b, i: (layer, row0 + b, 0, 0))


def _slab_shape(b, s):
    return jax.ShapeDtypeStruct((b, N_SLAB, s // ROW_TILE * TILE_PITCH, LANES), F32)


def _inproj_kernel(x_ref, mod_ref, g_ref, w_ref, ua_ref, uv_ref, ur_ref, gg_ref, *, n1):
    shift = mod_ref[:, 0:D_MODEL]
    scale = mod_ref[:, D_MODEL:2 * D_MODEL]
    h = _rms_norm(x_ref[...], g_ref[...] * (1.0 + scale)) + shift
    z = jnp.dot(h.astype(BF16), w_ref[...], preferred_element_type=F32)
    n_grp = n1 // SUBLANES
    for c in range(N_SLAB):
        for s2 in range(IN_TILE // n1):
            for grp in range(n_grp):
                src = slice(s2 * n1 + grp * SUBLANES, s2 * n1 + (grp + 1) * SUBLANES)
                dst = slice(s2 * SUBLANES, (s2 + 1) * SUBLANES)
                ua_ref[c, grp, dst, :] = z[src, COL_A + c * LANES:COL_A + (c + 1) * LANES]
                uv_ref[c, grp, dst, :] = z[src, COL_V + c * LANES:COL_V + (c + 1) * LANES]
    pad = jnp.zeros((SEG_PITCH - SEG, LANES), F32)
    for c in range(N_SLAB):
        ur = z[:, COL_R + c * LANES:COL_R + (c + 1) * LANES]
        for j in range(IN_TILE // SEG):
            ur_ref[c, j * SEG_PITCH:j * SEG_PITCH + SEG, :] = ur[j * SEG:(j + 1) * SEG]
            ur_ref[c, j * SEG_PITCH + SEG:(j + 1) * SEG_PITCH, :] = pad
    gg_ref[...] = _gelu_tanh(z[:, COL_G:COL_G + D_LRU]).astype(BF16)


def _in_projection(x, mod, layer, row0, g_mix, w_fold):
    b, s, _ = x.shape
    n_out = w_fold.shape[-1]
    slab = _slab_shape(b, s)
    slab_spec = pl.BlockSpec((None, N_SLAB, IN_TILE // SEG * SEG_PITCH, LANES), lambda bi, i: (bi, 0, i, 0))
    n1 = s // DFT_N2
    n_grp = n1 // SUBLANES
    grp_rows = IN_TILE // n1 * SUBLANES
    half = jax.ShapeDtypeStruct((b, N_SLAB, n_grp, DFT_N2 * SUBLANES, LANES), F32)
    half_spec = pl.BlockSpec((None, N_SLAB, n_grp, grp_rows, LANES), lambda bi, i: (bi, 0, 0, i, 0))
    return pl.pallas_call(
        functools.partial(_inproj_kernel, n1=n1),
        grid=(b, s // IN_TILE),
        in_specs=[
            pl.BlockSpec((None, IN_TILE, D_MODEL), lambda bi, i: (bi, i, 0)),
            _mod_spec(layer, row0),
            _const_spec((1, D_MODEL)),
            _layer_spec(layer, (D_MODEL, n_out)),
        ],
        out_specs=[half_spec, half_spec, slab_spec,
                   pl.BlockSpec((None, IN_TILE, D_LRU), lambda bi, i: (bi, i, 0))],
        out_shape=[half, half, slab, jax.ShapeDtypeStruct((b, s, D_LRU), BF16)],
        compiler_params=_params("parallel", "parallel"),
        name="inproj",
    )(x, mod, g_mix.reshape(1, D_MODEL), w_fold)


def _dft_tables(seq):
    n2 = DFT_N2
    n1 = seq // n2
    k2 = np.arange(n2)
    ang = 2.0 * np.pi * np.outer(k2, k2) / n2
    c2 = np.cos(ang) / math.sqrt(n2)
    s2 = np.sin(ang) / math.sqrt(n2)
    fa = np.concatenate([c2, -s2], axis=0)
    fv = np.concatenate([s2, c2], axis=0)
    g = DFT_GROUP
    n_groups = n2 // g
    rows = n1 * g
    k1 = np.arange(n1)
    s1 = np.arange(n1)
    kp = np.arange(g)
    scale = 1.0 / math.sqrt(n1 * HEAD_DIM)
    lc = np.zeros((n_groups, n1, g, n1, g), np.float64)
    ls = np.zeros((n_groups, n1, g, n1, g), np.float64)
    for grp in range(n_groups):
        k = n2 * k1[:, None] + (g * grp + kp)[None, :]
        ang_b = 2.0 * np.pi * (k[:, :, None] * s1[None, None, :] % seq) / seq
        for q in range(g):
            lc[grp, :, q, :, q] = np.cos(ang_b[:, q, :]) * scale
            ls[grp, :, q, :, q] = np.sin(ang_b[:, q, :]) * scale
    lc = lc.reshape(n_groups, rows, rows)
    ls = ls.reshape(n_groups, rows, rows)
    as_bf16 = lambda a: jnp.asarray(a.astype(np.float32)).astype(BF16)
    return as_bf16(fa), as_bf16(fv), as_bf16(lc), as_bf16(ls)


def _dft_kernel(a_ref, v_ref, fa_ref, fv_ref, lc_ref, ls_ref, y_ref, tre_scr, tim_scr, *, n1, groups_per_step):
    k = pl.program_id(1)
    n_grp = n1 // SUBLANES

    @pl.when(k < n_grp)
    def _():
        for q in range(SUBLANES):
            rows = pl.ds(q, DFT_N2, stride=SUBLANES)
            a = jnp.concatenate([a_ref[c, rows, :] for c in range(N_SLAB)], axis=1).astype(BF16)
            v = jnp.concatenate([v_ref[c, rows, :] for c in range(N_SLAB)], axis=1).astype(BF16)
            o = (jnp.dot(fa_ref[...], a, preferred_element_type=F32)
                 + jnp.dot(fv_ref[...], v, preferred_element_type=F32))
            tre_scr[k * SUBLANES + q] = o[0:DFT_N2].astype(BF16)
            tim_scr[k * SUBLANES + q] = o[DFT_N2:].astype(BF16)

    @pl.when(k >= n_grp)
    def _():
        rows = n1 * DFT_GROUP
        for j in range(groups_per_step):
            first = pl.multiple_of(((k - n_grp) * groups_per_step + j) * DFT_GROUP, DFT_GROUP)
            tr = tre_scr[:, pl.ds(first, DFT_GROUP), :].reshape(rows, D_FOURIER)
            ti = tim_scr[:, pl.ds(first, DFT_GROUP), :].reshape(rows, D_FOURIER)
            o = (jnp.dot(lc_ref[j], tr, preferred_element_type=F32)
                 + jnp.dot(ls_ref[j], ti, preferred_element_type=F32))
            y_ref[:, j * DFT_GROUP:(j + 1) * DFT_GROUP, :] = o.astype(BF16).reshape(n1, DFT_GROUP, D_FOURIER)


def _fourier_mix(ua, uv, s):
    b = ua.shape[0]
    n2 = DFT_N2
    n1 = s // n2
    n_grp = n1 // SUBLANES
    rows = n1 * DFT_GROUP
    groups_per_step = DFT_ROWS_PER_STEP // rows
    n_out = n2 // (DFT_GROUP * groups_per_step)
    fa, fv, lc, ls = _dft_tables(s)
    in_spec = pl.BlockSpec((None, N_SLAB, None, n2 * SUBLANES, LANES),
                           lambda bi, k: (bi, 0, jnp.minimum(k, n_grp - 1), 0, 0))
    l_spec = pl.BlockSpec((groups_per_step, rows, rows), lambda bi, k: (jnp.maximum(k - n_grp, 0), 0, 0))
    t_scr = pltpu.VMEM((n1, n2, D_FOURIER), BF16)
    y = pl.pallas_call(
        functools.partial(_dft_kernel, n1=n1, groups_per_step=groups_per_step),
        grid=(b, n_grp + n_out),
        in_specs=[in_spec, in_spec, _const_spec((2 * n2, n2)), _const_spec((2 * n2, n2)), l_spec, l_spec],
        out_specs=pl.BlockSpec((None, n1, DFT_GROUP * groups_per_step, D_FOURIER),
                               lambda bi, k: (bi, 0, jnp.maximum(k - n_grp, 0), 0)),
        out_shape=jax.ShapeDtypeStruct((b, n1, n2, D_FOURIER), BF16),
        scratch_shapes=[t_scr, t_scr],
        compiler_params=_params("parallel", "arbitrary"),
        name="seq_dft",
    )(ua, uv, fa, fv, lc, ls)
    return y.reshape(b, s, D_FOURIER)


def _lru_kernel(ur_ref, prev_ref, next_ref, cw_ref, cb_ref, wg_ref, bg_ref, lam_ref,
                st_ref, ah_ref, uc_scr, pre_scr, hf_scr):
    i = pl.program_id(1)
    n_steps = pl.num_programs(1)
    sub = lax.broadcasted_iota(jnp.int32, (SUBLANES, LANES), 0)
    slab_lanes = [slice(c * LANES, (c + 1) * LANES) for c in range(N_SLAB)]

    cw = [[jnp.broadcast_to(0.5 * cw_ref[k:k + 1, slab_lanes[c]], (SUBLANES, LANES)) for k in range(CONV_WIDTH)]
          for c in range(N_SLAB)]
    cbias = [jnp.broadcast_to(0.5 * cb_ref[:, slab_lanes[c]], (SUBLANES, LANES)) for c in range(N_SLAB)]
    neg_lam = -lam_ref[...]
    softplus = jnp.maximum(neg_lam, 0.0) + jnp.log1p(jnp.exp(-jnp.abs(neg_lam)))
    k_all = (-0.5 * LRU_C * LOG2E) * softplus
    k_rows = [[jnp.broadcast_to(k_all[d:d + 1, slab_lanes[c]], (SUBLANES, LANES)) for c in range(N_SLAB)]
              for d in range(2)]
    pad = jnp.zeros((SEG_PITCH - SEG, LANES), F32)

    def scan_inputs(rows, c, d):
        col = (c * LANES // MXU_K) * GATE_COLS + d * 2 * MXU_K + (c * LANES) % MXU_K
        th_r = jnp.tanh(pre_scr[rows, col:col + LANES])
        th_i = jnp.tanh(pre_scr[rows, col + MXU_K:col + MXU_K + LANES])
        k = k_rows[d][c]
        a = jnp.exp2(k * th_r + k)
        m = 1.0 - a * a
        mult = jnp.where(m > 0.0, m * lax.rsqrt(m), 0.0)
        return a, mult * ((th_i + 1.0) * uc_scr[rows, slab_lanes[c]])

    for u in range(LRU_TILES_PER_STEP):
        base = u * TILE_PITCH
        last_seg_end = base - SEG_PITCH + SEG

        def segment_rows(ref, c, t, base=base):
            return ref[c, pl.ds(base + t, SUBLANES, stride=SEG_PITCH), :]

        for c in range(N_SLAB):
            for j in range(SUBLANES):
                pad_rows = slice(base + j * SEG_PITCH + SEG, base + (j + 1) * SEG_PITCH)
                for plane in range(N_STATE):
                    st_ref[plane, c, pad_rows, :] = pad

        for c in range(N_SLAB):
            lanes = slab_lanes[c]
            if u == 0:
                prev_rows = [jnp.where(i > 0, prev_ref[c, SUBLANES - 2 + q:SUBLANES - 1 + q, :], 0.0) for q in range(2)]
            else:
                prev_rows = [ur_ref[c, last_seg_end - 2 + q:last_seg_end - 1 + q, :] for q in range(2)]
            if u == LRU_TILES_PER_STEP - 1:
                next_row = jnp.where(i < n_steps - 1, next_ref[c, 0:1, :], 0.0)
            else:
                next_row = ur_ref[c, base + TILE_PITCH:base + TILE_PITCH + 1, :]
            first = segment_rows(ur_ref, c, 0)
            window = [jnp.where(sub == 0, prev_rows[q], pltpu.roll(segment_rows(ur_ref, c, SEG - 2 + q), 1, 0))
                      for q in range(2)] + [first]
            for t in range(SEG):
                if t + 1 < SEG:
                    nxt = segment_rows(ur_ref, c, t + 1)
                else:
                    nxt = jnp.where(sub == SUBLANES - 1, next_row, pltpu.roll(first, SUBLANES - 1, 0))
                window.append(nxt)
                uc = (cbias[c] + window[0] * cw[c][0] + window[1] * cw[c][1] + window[2] * cw[c][2]
                      + window[3] * cw[c][3])
                uc_scr[t * SUBLANES:(t + 1) * SUBLANES, lanes] = uc
                window.pop(0)

        for hh in range(D_LRU // MXU_K):
            cols = slice(hh * GATE_COLS, (hh + 1) * GATE_COLS)
            lhs = uc_scr[:, hh * MXU_K:(hh + 1) * MXU_K].astype(BF16)
            pre_scr[:, cols] = jnp.dot(lhs, wg_ref[hh], preferred_element_type=F32) + bg_ref[:, cols]

        segs = slice(u * SUBLANES, (u + 1) * SUBLANES)
        hs = [jnp.zeros((SUBLANES, LANES), F32) for _ in range(N_SLAB)]
        ps = [jnp.ones((SUBLANES, LANES), F32) for _ in range(N_SLAB)]
        for t in range(SEG):
            rows = slice(t * SUBLANES, (t + 1) * SUBLANES)
            for c in range(N_SLAB):
                a, b = scan_inputs(rows, c, 0)
                hs[c] = a * hs[c] + b
                ps[c] = a * ps[c]
                hf_scr[rows, slab_lanes[c]] = hs[c]
                st_ref[1, c, pl.ds(base + t, SUBLANES, stride=SEG_PITCH), :] = ps[c]
        for c in range(N_SLAB):
            ah_ref[0, segs, slab_lanes[c]] = ps[c]
            ah_ref[1, segs, slab_lanes[c]] = hs[c]

        hs = [jnp.zeros((SUBLANES, LANES), F32) for _ in range(N_SLAB)]
        ps = [jnp.ones((SUBLANES, LANES), F32) for _ in range(N_SLAB)]
        for t in reversed(range(SEG)):
            rows = slice(t * SUBLANES, (t + 1) * SUBLANES)
            for c in range(N_SLAB):
                a, b = scan_inputs(rows, c, 1)
                hs[c] = a * hs[c] + b
                ps[c] = a * ps[c]
                st_ref[0, c, pl.ds(base + t, SUBLANES, stride=SEG_PITCH), :] = hf_scr[rows, slab_lanes[c]] + hs[c]
                st_ref[2, c, pl.ds(base + t, SUBLANES, stride=SEG_PITCH), :] = ps[c]
        for c in range(N_SLAB):
            ah_ref[2, segs, slab_lanes[c]] = ps[c]
            ah_ref[3, segs, slab_lanes[c]] = hs[c]


def _lru_local(ur, layer, conv_w, conv_b, wg, bg, lam):
    b, _, rows, _ = ur.shape
    step_rows = LRU_TILES_PER_STEP * TILE_PITCH
    n_steps = rows // step_rows
    blocks_per_step = step_rows // SUBLANES
    last_block = rows // SUBLANES - 1
    s = rows // TILE_PITCH * ROW_TILE
    slab = _slab_shape(b, s)
    slab_spec = pl.BlockSpec((None, N_SLAB, step_rows, LANES), lambda bi, i: (bi, 0, i, 0))
    prev_spec = pl.BlockSpec((None, N_SLAB, SUBLANES, LANES),
                             lambda bi, i: (bi, 0, jnp.maximum(i * blocks_per_step - 2, 0), 0))
    next_spec = pl.BlockSpec((None, N_SLAB, SUBLANES, LANES),
                             lambda bi, i: (bi, 0, jnp.minimum((i + 1) * blocks_per_step, last_block), 0))
    n_seg = s // SEG
    n_half = D_LRU // MXU_K
    tile_f32 = pltpu.VMEM((ROW_TILE, D_LRU), F32)
    return pl.pallas_call(
        _lru_kernel,
        grid=(b, n_steps),
        in_specs=[
            slab_spec, prev_spec, next_spec,
            _const_spec((CONV_WIDTH, D_LRU)),
            _const_spec((1, D_LRU)),
            _layer_spec(layer, (n_half, MXU_K, GATE_COLS)),
            _const_spec((1, n_half * GATE_COLS)),
            _const_spec((2, D_LRU)),
        ],
        out_specs=[pl.BlockSpec((None, N_STATE, N_SLAB, step_rows, LANES), lambda bi, i: (bi, 0, 0, i, 0)),
                   pl.BlockSpec((None, N_STATS, LRU_TILES_PER_STEP * SUBLANES, D_LRU),
                                lambda bi, i: (bi, 0, i, 0))],
        out_shape=[jax.ShapeDtypeStruct((b, N_STATE) + slab.shape[1:], F32),
                   jax.ShapeDtypeStruct((b, N_STATS, n_seg, D_LRU), F32)],
        scratch_shapes=[tile_f32, pltpu.VMEM((ROW_TILE, n_half * GATE_COLS), F32), tile_f32],
        compiler_params=_params("parallel", "parallel"),
        name="lru_local",
    )(ur, ur, ur, conv_w, conv_b.reshape(1, D_LRU), wg, bg, lam)


def _ffn_kernel(x_ref, yf_ref, gg_ref, st_ref, ah_ref, mod_ref, g_ref,
                wo_ref, w1_ref, w2_ref, gfin_ref, o_ref, yr_scr, cc_scr, *, final_norm, n_seg):
    i = pl.program_id(1)

    @pl.when(i == 0)
    def _():
        def step(n, carry):
            cf, cb = carry
            sf = n
            sb = n_seg - 1 - n
            cc_scr[0, pl.ds(sf, 1), :] = cf
            cc_scr[1, pl.ds(sb, 1), :] = cb
            return (ah_ref[1, pl.ds(sf, 1), :] + ah_ref[0, pl.ds(sf, 1), :] * cf,
                    ah_ref[3, pl.ds(sb, 1), :] + ah_ref[2, pl.ds(sb, 1), :] * cb)

        zero = jnp.zeros((1, D_LRU), F32)
        lax.fori_loop(0, n_seg, step, (zero, zero), unroll=CARRY_UNROLL)

    tile_segs = pl.ds(pl.multiple_of(i * (ROW_TILE // SEG), ROW_TILE // SEG), ROW_TILE // SEG)
    carry_f = cc_scr[0, tile_segs, :]
    carry_b = cc_scr[1, tile_segs, :]
    for c in range(N_SLAB):
        lanes = slice(c * LANES, (c + 1) * LANES)
        for j in range(ROW_TILE // SEG):
            rows = slice(j * SEG_PITCH, j * SEG_PITCH + SEG)
            hsum = (st_ref[0, c, rows, :] + st_ref[1, c, rows, :] * carry_f[j:j + 1, lanes]
                    + st_ref[2, c, rows, :] * carry_b[j:j + 1, lanes])
            gate = gg_ref[j * SEG:(j + 1) * SEG, lanes].astype(F32)
            yr_scr[j * SEG:(j + 1) * SEG, lanes] = (hsum * gate).astype(BF16)
    gate1 = mod_ref[:, 2 * D_MODEL:3 * D_MODEL]
    shift2 = mod_ref[:, 3 * D_MODEL:4 * D_MODEL]
    scale2 = mod_ref[:, 4 * D_MODEL:5 * D_MODEL]
    gate2 = mod_ref[:, 5 * D_MODEL:6 * D_MODEL]
    mix = (jnp.dot(yf_ref[...], wo_ref[0:D_FOURIER, :], preferred_element_type=F32)
           + jnp.dot(yr_scr[...], wo_ref[D_FOURIER:, :], preferred_element_type=F32))
    x1 = x_ref[...] + gate1 * mix
    h = (_rms_norm(x1, g_ref[...] * (1.0 + scale2)) + shift2).astype(BF16)
    acc = jnp.zeros((ROW_TILE, D_MODEL), F32)
    for k in range(D_FF // FF_CHUNK):
        cols = slice(k * FF_CHUNK, (k + 1) * FF_CHUNK)
        t = jnp.maximum(jnp.dot(h, w1_ref[:, cols], preferred_element_type=F32), 0.0)
        acc = acc + jnp.dot((t * t).astype(BF16), w2_ref[cols, :], preferred_element_type=F32)
    x2 = x1 + gate2 * acc
    if final_norm:
        x2 = _rms_norm(x2, gfin_ref[...])
    o_ref[...] = x2


def _out_and_mlp(x, yf, gg, st, ah, mod, layer, row0, g_mlp, w_out, w_ff1, w_ff2, g_final,
                 final_norm):
    b, s, _ = x.shape
    n_seg = s // SEG
    x_spec = pl.BlockSpec((None, ROW_TILE, D_MODEL), lambda bi, i: (bi, i, 0))
    st_spec = pl.BlockSpec((None, N_STATE, N_SLAB, TILE_PITCH, LANES), lambda bi, i: (bi, 0, 0, i, 0))
    ah_spec = pl.BlockSpec((None, N_STATS, n_seg, D_LRU), lambda bi, i: (bi, 0, 0, 0))
    return pl.pallas_call(
        functools.partial(_ffn_kernel, final_norm=final_norm, n_seg=n_seg),
        grid=(b, s // ROW_TILE),
        in_specs=[
            x_spec,
            pl.BlockSpec((None, ROW_TILE, D_FOURIER), lambda bi, i: (bi, i, 0)),
            pl.BlockSpec((None, ROW_TILE, D_LRU), lambda bi, i: (bi, i, 0)),
            st_spec, ah_spec,
            _mod_spec(layer, row0),
            _const_spec((1, D_MODEL)),
            _layer_spec(layer, (D_MODEL, D_MODEL)),
            _layer_spec(layer, (D_MODEL, D_FF)),
            _layer_spec(layer, (D_FF, D_MODEL)),
            _const_spec((1, D_MODEL)),
        ],
        out_specs=x_spec,
        out_shape=jax.ShapeDtypeStruct(x.shape, F32),
        scratch_shapes=[pltpu.VMEM((ROW_TILE, D_LRU), BF16), pltpu.VMEM((N_DIR, n_seg, D_LRU), F32)],
        compiler_params=_params("parallel", "arbitrary"),
        name="outproj_mlp",
    )(x, yf, gg, st, ah, mod, g_mlp.reshape(1, D_MODEL), w_out, w_ff1, w_ff2,
      g_final.reshape(1, D_MODEL))


def _gate_weights(w_rgate, b_rgate, w_igate, b_igate):
    depth = w_rgate.shape[0]
    n_half = D_LRU // MXU_K
    heads_per_half = N_HEADS // n_half
    w_halves, b_halves = [], []
    for hh in range(n_half):
        hs = slice(hh * heads_per_half, (hh + 1) * heads_per_half)
        w_cols, b_cols = [], []
        for d in range(2):
            for w, bias in ((w_rgate, b_rgate), (w_igate, b_igate)):
                w_cols.append(_block_diag(w[:, d, hs]))
                b_cols.append(bias[:, d, hs].reshape(depth, MXU_K))
        w_halves.append(jnp.concatenate(w_cols, axis=-1))
        b_halves.append(jnp.concatenate(b_cols, axis=-1))
    wg = jnp.stack(w_halves, axis=1).astype(BF16)
    bg = (0.5 * jnp.concatenate(b_halves, axis=-1)).reshape(depth, 1, n_half * GATE_COLS)
    return wg, bg


def _trunk(x, mod, row0, layers, g_final):
    depth = len(layers)
    for l, p in enumerate(layers):
        ua, uv, ur, gg = _in_projection(x, mod, l, row0, p["g_mix"], p["w_fold"])
        yf = _fourier_mix(ua, uv, x.shape[1])
        st, ah = _lru_local(ur, l, p["conv_w"], p["conv_b"], p["wg"], p["bg"], p["lam"])
        x = _out_and_mlp(x, yf, gg, st, ah, mod, l, row0, p["g_mlp"], p["w_out"],
                         p["w_ff1"], p["w_ff2"], g_final, final_norm=(l == depth - 1))
    return x


def kernel(x_prompt, x_sample, c_prompt, c_sample, g_mix, g_mlp, w_mod, b_mod, w_in, w_fourier,
           conv_w, conv_b, w_rgate, b_rgate, w_igate, b_igate, lru_lambda, w_out, w_ff1, w_ff2,
           g_final):
    depth = w_in.shape[0]
    n_prompt = c_prompt.shape[0]
    n_cond = n_prompt + c_sample.shape[0]
    pad = -n_cond % SUBLANES
    c_all = jnp.concatenate([c_prompt, c_sample, jnp.zeros((pad, D_MODEL), F32)], axis=0)
    mod = _modulation(c_all, w_mod, b_mod)
    mod = mod.reshape(depth, n_cond + pad, 1, N_MOD * D_MODEL)

    w_fold = _fold_in_weights(w_in, w_fourier)
    wg, bg = _gate_weights(w_rgate, b_rgate, w_igate, b_igate)
    w_out_b = w_out.astype(BF16)
    w_ff1_b = w_ff1.astype(BF16)
    w_ff2_b = w_ff2.astype(BF16)
    layers = [dict(g_mix=g_mix[l], g_mlp=g_mlp[l], w_fold=w_fold, conv_w=conv_w[l], conv_b=conv_b[l],
                   wg=wg, bg=bg[l], lam=lru_lambda[l], w_out=w_out_b, w_ff1=w_ff1_b,
                   w_ff2=w_ff2_b) for l in range(depth)]
    y_prompt = _trunk(x_prompt, mod, 0, layers, g_final)
    y_sample = _trunk(x_sample, mod, n_prompt, layers, g_final)
    return (y_prompt, y_sample)
```
